```python
import math
import jax, jax.numpy as jnp
from jax import lax
import numpy as np

D_MODEL = 1024
BATCH = 2
SEQ = 8192
DEPTH = 4

GRID_W = 64
CTX_LEN = 256

N_ATTN_LAYERS = (DEPTH + 1) // 2
N_SSM_LAYERS = DEPTH // 2

ATTN_WIDTH = D_MODEL // 2
HEAD_DIM = 64
N_Q_HEADS = ATTN_WIDTH // HEAD_DIM
N_KV_HEADS = 2
Q_PER_KV = N_Q_HEADS // N_KV_HEADS
KV_WIDTH = N_KV_HEADS * HEAD_DIM
Q_BLOCK = 128
ROPE_THETA = 10000.0
GMLP_WIDTH = D_MODEL // 2
GMLP_CHUNK = 128
GMLP_GROUPS = 4
GMLP_GROUP_DIM = GMLP_WIDTH // GMLP_GROUPS
MIX_IN = ATTN_WIDTH + 2 * KV_WIDTH + 2 * GMLP_WIDTH
MIX_OUT = ATTN_WIDTH + GMLP_WIDTH
SSM_INNER = 2 * D_MODEL
SSM_HEAD_DIM = 64
SSM_HEADS = SSM_INNER // SSM_HEAD_DIM
SSM_GROUPS = 4
SSM_STATE = 128
SSM_CONV = 3
SSM_CHUNK = 128
SSM_BC_WIDTH = SSM_GROUPS * SSM_STATE
SSM_CONV_DIM = SSM_INNER + 2 * SSM_BC_WIDTH
SSM_IN = SSM_INNER + SSM_CONV_DIM + 2 * SSM_HEADS
FFN_HIDDEN = -(-8 * D_MODEL // (3 * 256)) * 256
NORM_EPS = 1e-6

kernel_name = "hybrid_dit_gmlp_gqa_ssd_block"


def rms_norm(x, g):
    xf = x.astype(jnp.float32)
    y = xf * lax.rsqrt(jnp.mean(xf * xf, axis=-1, keepdims=True) + NORM_EPS)
    return (y * g.astype(jnp.float32)).astype(x.dtype)


def adaln(cond, w, b):
    return jnp.split(jax.nn.silu(cond) @ w + b, 6, axis=-1)


def modulate(h, shift, scale):
    return h * (1 + scale) + shift


def axial_rope_tables(rows):
    t = jnp.arange(rows * GRID_W)
    row = (t // GRID_W).astype(jnp.float32)
    col = (t % GRID_W).astype(jnp.float32)
    n_freq = HEAD_DIM // 4
    inv = ROPE_THETA ** (-jnp.arange(n_freq, dtype=jnp.float32) / n_freq)
    ang = jnp.concatenate([row[:, None] * inv, col[:, None] * inv], axis=-1)
    return jnp.cos(ang), jnp.sin(ang)


def apply_rope(x, cos, sin):
    half = HEAD_DIM // 2
    x1 = x[..., :half].astype(jnp.float32)
    x2 = x[..., half:].astype(jnp.float32)
    cs = cos[None, :, None, :]
    sn = sin[None, :, None, :]
    return jnp.concatenate([x1 * cs - x2 * sn, x2 * cs + x1 * sn], axis=-1).astype(x.dtype)


def gqa_attend(q, k, v):
    s = jnp.einsum("bqkgd,btkd->bkgqt", q, k).astype(jnp.float32) * (HEAD_DIM ** -0.5)
    p = jax.nn.softmax(s, axis=-1).astype(v.dtype)
    return jnp.einsum("bkgqt,btkd->bqkgd", p, v)


def latent_attention(q, k, v, k_ctx, v_ctx):
    b, s = q.shape[:2]
    k_all = jnp.concatenate([k_ctx, k], axis=1)
    v_all = jnp.concatenate([v_ctx, v], axis=1)
    qb = q.reshape(b, s // Q_BLOCK, Q_BLOCK, N_KV_HEADS, Q_PER_KV, HEAD_DIM).transpose(1, 0, 2, 3, 4, 5)
    o = lax.map(lambda qi: gqa_attend(qi, k_all, v_all), qb)
    return o.transpose(1, 0, 2, 3, 4, 5).reshape(b, s, ATTN_WIDTH)


def spatial_gating(u, v, norm_g, w_s, b_s):
    b, l, _ = u.shape
    vn = rms_norm(v.reshape(b, l, GMLP_GROUPS, GMLP_GROUP_DIM), norm_g.reshape(GMLP_GROUPS, GMLP_GROUP_DIM))
    vc = vn.reshape(b, l // GMLP_CHUNK, GMLP_CHUNK, GMLP_GROUPS, GMLP_GROUP_DIM)
    s = jnp.einsum("gpq,bnqgc->bnpgc", w_s, vc) + b_s.T[None, None, :, :, None]
    return u * s.reshape(b, l, GMLP_WIDTH)


def attn_gmlp_mixer(hx, hc, w_in, w_out, q_g, k_g, sgu_g, sgu_w, sgu_b, cos, sin):
    def project(h):
        b, l = h.shape[:2]
        p = h @ w_in
        q, k, v, gm = jnp.split(p, [ATTN_WIDTH, ATTN_WIDTH + KV_WIDTH, ATTN_WIDTH + 2 * KV_WIDTH], axis=-1)
        q = rms_norm(q.reshape(b, l, N_Q_HEADS, HEAD_DIM), q_g)
        k = rms_norm(k.reshape(b, l, N_KV_HEADS, HEAD_DIM), k_g)
        v = v.reshape(b, l, N_KV_HEADS, HEAD_DIM)
        gu, gv = jnp.split(jax.nn.gelu(gm), 2, axis=-1)
        return q, k, v, gu, gv

    b, l = hx.shape[:2]
    lc = hc.shape[1]
    qx, kx, vx, ux, gx = project(hx)
    qc, kc, vc, uc, gc = project(hc)
    qx = apply_rope(qx, cos, sin)
    kx = apply_rope(kx, cos, sin)
    att_x = latent_attention(qx.reshape(b, l, N_KV_HEADS, Q_PER_KV, HEAD_DIM), kx, vx, kc, vc)
    att_c = gqa_attend(qc.reshape(b, lc, N_KV_HEADS, Q_PER_KV, HEAD_DIM), kc, vc).reshape(b, lc, ATTN_WIDTH)
    sg_x = spatial_gating(ux, gx, sgu_g, sgu_w, sgu_b)
    sg_c = spatial_gating(uc, gc, sgu_g, sgu_w, sgu_b)
    out_x = jnp.concatenate([att_x, sg_x], axis=-1) @ w_out
    out_c = jnp.concatenate([att_c, sg_c], axis=-1) @ w_out
    return out_x, out_c


def depthwise_conv(x, w, b):
    k = w.shape[0]
    y = lax.conv_general_dilated(x, w[:, None, :].astype(x.dtype), window_strides=(1,),
                                 padding=[((k - 1) // 2, k // 2)],
                                 dimension_numbers=("NWC", "WIO", "NWC"),
                                 feature_group_count=x.shape[-1])
    return y + b


def ssd_scan(x, dt, a, bm, cm, h0):
    b, l, h, p = x.shape
    g, n = bm.shape[2], bm.shape[3]
    r = h // g
    nc = l // SSM_CHUNK
    L = SSM_CHUNK
    xd = (x * dt[..., None]).reshape(b, nc, L, g, r, p)
    da = (dt * a).reshape(b, nc, L, g, r).transpose(0, 3, 4, 1, 2)
    bc = bm.reshape(b, nc, L, g, n)
    cc = cm.reshape(b, nc, L, g, n)
    a_cs = jnp.cumsum(da, axis=-1)
    lower = jnp.tril(jnp.ones((L, L), dtype=bool))
    seg = jnp.where(lower, a_cs[..., :, None] - a_cs[..., None, :], -jnp.inf)
    decay_in = jnp.exp(seg)
    cb = jnp.einsum("bclgn,bcsgn->bgcls", cc, bc)
    y_diag = jnp.einsum("bgcls,bgrcls,bcsgrp->bclgrp", cb, decay_in, xd)
    decay_to_end = jnp.exp(a_cs[..., -1:] - a_cs)
    chunk_states = jnp.einsum("bcsgn,bgrcs,bcsgrp->cbgrpn", bc, decay_to_end, xd)
    chunk_decay = jnp.exp(a_cs[..., -1]).transpose(3, 0, 1, 2)

    def step(state, inp):
        s_c, dec = inp
        return state * dec[..., None, None] + s_c, state

    h_final, h_start = lax.scan(step, h0.reshape(b, g, r, p, n), (chunk_states, chunk_decay))
    y_off = jnp.einsum("bclgn,cbgrpn,bgrcl->bclgrp", cc, h_start, jnp.exp(a_cs))
    y = (y_diag + y_off).reshape(b, l, h, p)
    return y, h_final.reshape(b, h, p, n)


def maybe_flip(t, rev):
    return jnp.flip(t, axis=1) if rev else t


def ssd_mixer(hx, hc, w_in, conv_w, conv_b, dt_bias, a_log, d_skip, norm_g, w_out):
    a = -jnp.exp(a_log.astype(jnp.float32))

    def prep(h):
        b, l = h.shape[:2]
        p = h @ w_in
        z, xbc, dt = jnp.split(p, [SSM_INNER, SSM_INNER + SSM_CONV_DIM], axis=-1)
        xbc = jax.nn.silu(depthwise_conv(xbc, conv_w, conv_b))
        xs, bm, cm = jnp.split(xbc, [SSM_INNER, SSM_INNER + SSM_BC_WIDTH], axis=-1)
        xs = xs.reshape(b, l, SSM_HEADS, SSM_HEAD_DIM).astype(jnp.float32)
        bm = bm.reshape(b, l, SSM_GROUPS, SSM_STATE).astype(jnp.float32)
        cm = cm.reshape(b, l, SSM_GROUPS, SSM_STATE).astype(jnp.float32)
        dt = jax.nn.softplus(dt.astype(jnp.float32).reshape(b, l, 2, SSM_HEADS) + dt_bias.astype(jnp.float32))
        return z, xs, bm, cm, dt

    zx, xx, bx, cx, dtx = prep(hx)
    zc, xc, bc, cc, dtc = prep(hc)
    b = hx.shape[0]
    d = d_skip.astype(jnp.float32)[:, None]
    y_x = d * xx
    y_c = d * xc
    for direction in range(2):
        rev = direction == 1
        h0 = jnp.zeros((b, SSM_HEADS, SSM_HEAD_DIM, SSM_STATE), jnp.float32)
        yc_d, hc_d = ssd_scan(maybe_flip(xc, rev), maybe_flip(dtc[:, :, direction], rev), a[direction],
                              maybe_flip(bc, rev), maybe_flip(cc, rev), h0)
        yx_d, _ = ssd_scan(maybe_flip(xx, rev), maybe_flip(dtx[:, :, direction], rev), a[direction],
                           maybe_flip(bx, rev), maybe_flip(cx, rev), hc_d)
        y_c = y_c + maybe_flip(yc_d, rev)
        y_x = y_x + maybe_flip(yx_d, rev)

    def finish(y, z):
        b_, l_ = z.shape[:2]
        y = y.reshape(b_, l_, SSM_INNER) * jax.nn.silu(z.astype(jnp.float32))
        y = rms_norm(y.reshape(b_, l_, SSM_GROUPS, SSM_INNER // SSM_GROUPS),
                     norm_g.reshape(SSM_GROUPS, SSM_INNER // SSM_GROUPS)).reshape(b_, l_, SSM_INNER)
        return y.astype(z.dtype) @ w_out

    return finish(y_x, zx), finish(y_c, zc)


def swiglu(h, w_in, w_out):
    g, u = jnp.split(h @ w_in, 2, axis=-1)
    return (jax.nn.silu(g) * u) @ w_out


def setup_inputs(seed: int = 0) -> dict:
    key = jax.random.key(seed)
    ks = jax.random.split(key, 28)
    f32 = jnp.float32

    def nrm(k, shape, fan_in):
        return jax.random.normal(k, shape, f32) * fan_in ** -0.5

    def gain(k, shape):
        return 1.0 + 0.05 * jax.random.normal(k, shape, f32)

    dt0 = jnp.exp(jax.random.uniform(ks[22], (N_SSM_LAYERS, 2, SSM_HEADS), f32, math.log(1e-3), math.log(1e-1)))
    return {
        "x": jax.random.normal(ks[0], (BATCH, SEQ, D_MODEL), f32),
        "c": jax.random.normal(ks[1], (BATCH, D_MODEL), f32),
        "ctx": jax.random.normal(ks[2], (BATCH, CTX_LEN, D_MODEL), f32),
        "c_ctx": jax.random.normal(ks[3], (D_MODEL,), f32),
        "ada_w": nrm(ks[4], (DEPTH, D_MODEL, 6 * D_MODEL), D_MODEL),
        "ada_b": 0.02 * jax.random.normal(ks[5], (DEPTH, 6 * D_MODEL), f32),
        "norm1_g": gain(ks[6], (DEPTH, D_MODEL)),
        "norm2_g": gain(ks[7], (DEPTH, D_MODEL)),
        "ffn_w_in": nrm(ks[8], (DEPTH, D_MODEL, 2 * FFN_HIDDEN), D_MODEL),
        "ffn_w_out": nrm(ks[9], (DEPTH, FFN_HIDDEN, D_MODEL), FFN_HIDDEN),
        "mix_w_in": nrm(ks[10], (N_ATTN_LAYERS, D_MODEL, MIX_IN), D_MODEL),
        "mix_w_out": nrm(ks[11], (N_ATTN_LAYERS, MIX_OUT, D_MODEL), MIX_OUT),
        "q_norm_g": gain(ks[12], (N_ATTN_LAYERS, HEAD_DIM)),
        "k_norm_g": gain(ks[13], (N_ATTN_LAYERS, HEAD_DIM)),
        "sgu_norm_g": gain(ks[14], (N_ATTN_LAYERS, GMLP_WIDTH)),
        "sgu_w": nrm(ks[15], (N_ATTN_LAYERS, GMLP_GROUPS, GMLP_CHUNK, GMLP_CHUNK), GMLP_CHUNK),
        "sgu_b": gain(ks[16], (N_ATTN_LAYERS, GMLP_GROUPS, GMLP_CHUNK)),
        "ssm_w_in": nrm(ks[17], (N_SSM_LAYERS, D_MODEL, SSM_IN), D_MODEL),
        "ssm_conv_w": nrm(ks[18], (N_SSM_LAYERS, SSM_CONV, SSM_CONV_DIM), SSM_CONV),
        "ssm_conv_b": 0.02 * jax.random.normal(ks[19], (N_SSM_LAYERS, SSM_CONV_DIM), f32),
        "ssm_dt_bias": dt0 + jnp.log(-jnp.expm1(-dt0)),
        "ssm_a_log": jnp.log(jax.random.uniform(ks[20], (N_SSM_LAYERS, 2, SSM_HEADS), f32, 1.0, 16.0)),
        "ssm_d": gain(ks[21], (N_SSM_LAYERS, SSM_HEADS)),
        "ssm_norm_g": gain(ks[23], (N_SSM_LAYERS, SSM_INNER)),
        "ssm_w_out": nrm(ks[24], (N_SSM_LAYERS, SSM_INNER, D_MODEL), SSM_INNER),
    }


def reference(x, c, ctx, c_ctx, ada_w, ada_b, norm1_g, norm2_g, ffn_w_in, ffn_w_out,
              mix_w_in, mix_w_out, q_norm_g, k_norm_g, sgu_norm_g, sgu_w, sgu_b,
              ssm_w_in, ssm_conv_w, ssm_conv_b, ssm_dt_bias, ssm_a_log, ssm_d, ssm_norm_g, ssm_w_out):
    rows = x.shape[1] // GRID_W
    cos, sin = axial_rope_tables(rows)
    h_ctx = ctx
    cond_x = c[:, None, :]
    cond_c = c_ctx[None, None, :]
    for i in range(DEPTH):
        last = i == DEPTH - 1
        mx = adaln(cond_x, ada_w[i], ada_b[i])
        mc = adaln(cond_c, ada_w[i], ada_b[i])
        hx = modulate(rms_norm(x, norm1_g[i]), mx[0], mx[1])
        hc = modulate(rms_norm(h_ctx, norm1_g[i]), mc[0], mc[1])
        j = i // 2
        if i % 2 == 0:
            ox, oc = attn_gmlp_mixer(hx, hc, mix_w_in[j], mix_w_out[j], q_norm_g[j], k_norm_g[j],
                                     sgu_norm_g[j], sgu_w[j], sgu_b[j], cos, sin)
        else:
            ox, oc = ssd_mixer(hx, hc, ssm_w_in[j], ssm_conv_w[j], ssm_conv_b[j], ssm_dt_bias[j],
                               ssm_a_log[j], ssm_d[j], ssm_norm_g[j], ssm_w_out[j])
        x = x + mx[2] * ox
        x = x + mx[5] * swiglu(modulate(rms_norm(x, norm2_g[i]), mx[3], mx[4]), ffn_w_in[i], ffn_w_out[i])
        if not last:
            h_ctx = h_ctx + mc[2] * oc
            h_ctx = h_ctx + mc[5] * swiglu(modulate(rms_norm(h_ctx, norm2_g[i]), mc[3], mc[4]), ffn_w_in[i], ffn_w_out[i])
    return x
```

```python
import functools
import math

import jax
import jax.numpy as jnp
import numpy as np
from jax import lax
from jax.experimental import pallas as pl
from jax.experimental.pallas import tpu as pltpu

F32 = jnp.float32
BF16 = jnp.bfloat16

D_MODEL = 1024
GRID_W = 64
HEAD_DIM = 64
ATTN_WIDTH = D_MODEL // 2
N_Q_HEADS = ATTN_WIDTH // HEAD_DIM
N_KV_HEADS = 2
Q_PER_KV = N_Q_HEADS // N_KV_HEADS
KV_WIDTH = N_KV_HEADS * HEAD_DIM
ROPE_THETA = 10000.0
GMLP_WIDTH = D_MODEL // 2
GMLP_CHUNK = 128
GMLP_GROUPS = 4
MIX_IN = ATTN_WIDTH + 2 * KV_WIDTH + 2 * GMLP_WIDTH
SSM_INNER = 2 * D_MODEL
SSM_HEAD_DIM = 64
SSM_HEADS = SSM_INNER // SSM_HEAD_DIM
SSM_GROUPS = 4
SSM_STATE = 128
SSM_CHUNK = 128
SSM_BC_WIDTH = SSM_GROUPS * SSM_STATE
SSM_CONV_DIM = SSM_INNER + 2 * SSM_BC_WIDTH
FFN_HIDDEN = -(-8 * D_MODEL // (3 * 256)) * 256
NORM_EPS = 1e-6

LANES = 128
SUBLANES = 8
TOKEN_TILE = 256
VMEM_LIMIT = 56 * 1024 * 1024
LOG2E = 1.4426950408889634


def _cparams(n_axes):
    return pltpu.CompilerParams(dimension_semantics=("arbitrary",) * n_axes,
                                vmem_limit_bytes=VMEM_LIMIT)


def _silu(x):
    return x / (1.0 + jnp.exp(-x))


def _gelu_tanh(x):
    c = math.sqrt(2.0 / math.pi)
    return x * (0.5 * (1.0 + jnp.tanh(c * (x + 0.044715 * (x * x * x)))))


def _rms_mod(x, g, shift, scale):
    ms = jnp.mean(x * x, axis=-1, keepdims=True)
    y = (x * lax.rsqrt(ms + NORM_EPS)) * g
    return y * (1.0 + scale) + shift


def _split2(x):
    hi = x.astype(BF16)
    lo = (x - hi.astype(F32)).astype(BF16)
    return hi, lo


def _group_mean(x2, gmat):
    hi, lo = _split2(x2)
    return (jnp.dot(hi, gmat, preferred_element_type=F32)
            + jnp.dot(lo, gmat, preferred_element_type=F32))


def _rope(x, cos, sin_signed):
    n = x.shape[-1]
    half = HEAD_DIM // 2
    fwd = pltpu.roll(x, n - half, axis=1)
    bwd = pltpu.roll(x, half, axis=1)
    lane = lax.broadcasted_iota(jnp.int32, x.shape, 1)
    partner = jnp.where((lane % HEAD_DIM) < half, fwd, bwd)
    return x * cos + partner * sin_signed


def _adaln_kernel(c_ref, w_ref, b_ref, o_ref):
    s = _silu(c_ref[...]).astype(BF16)
    o_ref[...] = jnp.dot(s, w_ref[...].astype(BF16), preferred_element_type=F32) + b_ref[...]


def _adaln_all(cond, ada_w, ada_b):
    depth = ada_w.shape[0]
    tn = 1536
    return pl.pallas_call(
        _adaln_kernel,
        grid=(depth, 6 * D_MODEL // tn),
        in_specs=[pl.BlockSpec((SUBLANES, D_MODEL), lambda l, j: (0, 0)),
                  pl.BlockSpec((None, D_MODEL, tn), lambda l, j: (l, 0, j)),
                  pl.BlockSpec((None, 1, tn), lambda l, j: (l, 0, j))],
        out_specs=pl.BlockSpec((None, SUBLANES, tn), lambda l, j: (l, 0, j)),
        out_shape=jax.ShapeDtypeStruct((depth, SUBLANES, 6 * D_MODEL), F32),
        compiler_params=_cparams(2),
        name="adaln",
    )(cond, ada_w, ada_b.reshape(depth, 1, 6 * D_MODEL))


def _mod_spec(n_ctx_tiles, batch):
    return pl.BlockSpec((None, 1, 6 * D_MODEL),
                        lambda b, i: (jnp.where(i < n_ctx_tiles, batch, b), 0, 0))


def _const_spec(shape):
    nd = len(shape)
    return pl.BlockSpec(shape, lambda b, i: (0,) * nd)


def _mix_in_kernel(x_ref, mod_ref, g1_ref, w_ref, qg_ref, kg_ref, cos_ref, sin_ref, gq_ref, gk_ref,
                   sg_ref, qT_ref, k_ref, vT_ref, gu_ref, vn_ref):
    mod = mod_ref[...]
    h = _rms_mod(x_ref[...], g1_ref[...], mod[:, 0:D_MODEL], mod[:, D_MODEL:2 * D_MODEL]).astype(BF16)
    p = jnp.dot(h, w_ref[...], preferred_element_type=F32)
    q = p[:, 0:ATTN_WIDTH]
    k = p[:, ATTN_WIDTH:ATTN_WIDTH + KV_WIDTH]
    v = p[:, ATTN_WIDTH + KV_WIDTH:ATTN_WIDTH + 2 * KV_WIDTH]
    gm = p[:, ATTN_WIDTH + 2 * KV_WIDTH:]
    cos = cos_ref[...]
    sin = sin_ref[...]
    q = q * lax.rsqrt(_group_mean(q * q, gq_ref[...]) + NORM_EPS) * qg_ref[...]
    k = k * lax.rsqrt(_group_mean(k * k, gk_ref[...]) + NORM_EPS) * kg_ref[...]
    reps = ATTN_WIDTH // KV_WIDTH
    q = _rope(q, jnp.concatenate([cos] * reps, axis=1), jnp.concatenate([sin] * reps, axis=1))
    k = _rope(k, cos, sin)
    q = q * (HEAD_DIM ** -0.5 * LOG2E)
    qT_ref[...] = q.T.astype(BF16)
    k_ref[...] = k.astype(BF16)
    vT_ref[...] = v.T.astype(BF16)
    ge = _gelu_tanh(gm)
    gu_ref[...] = ge[:, 0:GMLP_WIDTH].astype(BF16)
    gv = ge[:, GMLP_WIDTH:]
    gd = GMLP_WIDTH // GMLP_GROUPS
    parts = []
    for g in range(GMLP_GROUPS):
        blk = gv[:, g * gd:(g + 1) * gd]
        ms = jnp.mean(blk * blk, axis=-1, keepdims=True)
        parts.append(blk * lax.rsqrt(ms + NORM_EPS))
    vn_ref[...] = (jnp.concatenate(parts, axis=1) * sg_ref[...]).astype(BF16)


def _mix_in(h, mods, g1, w_in, qg, kg, cos, sin, gq, gk, sgg, n_ctx_tiles):
    batch, t, _ = h.shape
    tm = TOKEN_TILE
    tok = lambda n: pl.BlockSpec((None, tm, n), lambda b, i: (b, i, 0))
    tr = lambda n: pl.BlockSpec((None, n, tm), lambda b, i: (b, 0, i))
    return pl.pallas_call(
        _mix_in_kernel,
        grid=(batch, t // tm),
        in_specs=[tok(D_MODEL), _mod_spec(n_ctx_tiles, batch), _const_spec((1, D_MODEL)),
                  _const_spec((D_MODEL, MIX_IN)), _const_spec((1, ATTN_WIDTH)), _const_spec((1, KV_WIDTH)),
                  pl.BlockSpec((tm, KV_WIDTH), lambda b, i: (i, 0)),
                  pl.BlockSpec((tm, KV_WIDTH), lambda b, i: (i, 0)),
                  _const_spec((ATTN_WIDTH, ATTN_WIDTH)), _const_spec((KV_WIDTH, KV_WIDTH)),
                  _const_spec((1, GMLP_WIDTH))],
        out_specs=[tr(ATTN_WIDTH), tok(KV_WIDTH), tr(KV_WIDTH), tok(GMLP_WIDTH), tok(GMLP_WIDTH)],
        out_shape=[jax.ShapeDtypeStruct((batch, ATTN_WIDTH, t), BF16),
                   jax.ShapeDtypeStruct((batch, t, KV_WIDTH), BF16),
                   jax.ShapeDtypeStruct((batch, KV_WIDTH, t), BF16),
                   jax.ShapeDtypeStruct((batch, t, GMLP_WIDTH), BF16),
                   jax.ShapeDtypeStruct((batch, t, GMLP_WIDTH), BF16)],
        compiler_params=_cparams(2),
        name="mix_in",
    )(h, mods, g1, w_in, qg, kg, cos, sin, gq, gk, sgg)


def _attn_kernel(qT_ref, k_ref, vT_ref, o_ref, qpad_ref, acc_ref, m_ref, l_ref, *, tq, tk, n_ctx_tiles,
                 nkv_ctx, nkv_all):
    i = pl.program_id(1)
    for kvh in range(N_KV_HEADS):
        qcat = jnp.concatenate(
            [qT_ref[(kvh * Q_PER_KV + g) * HEAD_DIM:(kvh * Q_PER_KV + g + 1) * HEAD_DIM, :]
             for g in range(Q_PER_KV)], axis=1)
        z = jnp.zeros_like(qcat)
        qpad_ref[kvh] = jnp.concatenate([qcat, z] if kvh == 0 else [z, qcat], axis=0)
    m_ref[...] = jnp.full(m_ref.shape, -jnp.inf, F32)
    l_ref[...] = jnp.zeros(l_ref.shape, F32)
    acc_ref[...] = jnp.zeros(acc_ref.shape, F32)
    nkv = jnp.where(i < n_ctx_tiles, nkv_ctx, nkv_all)

    def body(j, carry):
        off = pl.multiple_of(j * tk, tk)
        kblk = k_ref[pl.ds(off, tk), :]
        for kvh in range(N_KV_HEADS):
            s = jnp.dot(kblk, qpad_ref[kvh], preferred_element_type=F32)
            m_old = m_ref[kvh]
            m_new = jnp.maximum(m_old, jnp.max(s, axis=0, keepdims=True))
            alpha = jnp.exp2(m_old - m_new)
            p = jnp.exp2(s - m_new)
            l_ref[kvh] = alpha * l_ref[kvh] + jnp.sum(p, axis=0, keepdims=True)
            vblk = vT_ref[kvh * HEAD_DIM:(kvh + 1) * HEAD_DIM, pl.ds(off, tk)]
            pv = jnp.dot(vblk, p.astype(BF16), preferred_element_type=F32)
            acc_ref[kvh] = acc_ref[kvh] * alpha + pv
            m_ref[kvh] = m_new
        return carry

    lax.fori_loop(0, nkv, body, 0)
    for g in range(Q_PER_KV):
        og = jnp.concatenate(
            [acc_ref[kvh][:, g * tq:(g + 1) * tq] / l_ref[kvh][:, g * tq:(g + 1) * tq]
             for kvh in range(N_KV_HEADS)], axis=0)
        o_ref[:, g * KV_WIDTH:(g + 1) * KV_WIDTH] = og.T.astype(BF16)


def _attention(qT, k, vT, n_ctx_tiles, ctx_len):
    batch, _, t = qT.shape
    tq = TOKEN_TILE
    tk = 256
    kern = functools.partial(_attn_kernel, tq=tq, tk=tk, n_ctx_tiles=n_ctx_tiles,
                             nkv_ctx=ctx_len // tk, nkv_all=t // tk)
    return pl.pallas_call(
        kern,
        grid=(batch, t // tq),
        in_specs=[pl.BlockSpec((None, ATTN_WIDTH, tq), lambda b, i: (b, 0, i)),
                  pl.BlockSpec((None, t, KV_WIDTH), lambda b, i: (b, 0, 0)),
                  pl.BlockSpec((None, KV_WIDTH, t), lambda b, i: (b, 0, 0))],
        out_specs=pl.BlockSpec((None, tq, ATTN_WIDTH), lambda b, i: (b, i, 0)),
        out_shape=jax.ShapeDtypeStruct((batch, t, ATTN_WIDTH), BF16),
        scratch_shapes=[pltpu.VMEM((N_KV_HEADS, KV_WIDTH, Q_PER_KV * tq), BF16),
                        pltpu.VMEM((N_KV_HEADS, HEAD_DIM, Q_PER_KV * tq), F32),
                        pltpu.VMEM((N_KV_HEADS, 1, Q_PER_KV * tq), F32),
                        pltpu.VMEM((N_KV_HEADS, 1, Q_PER_KV * tq), F32)],
        compiler_params=_cparams(2),
        name="attention",
    )(qT, k, vT)


def _mix_out_kernel(att_ref, gu_ref, vn_ref, sw_ref, sb_ref, wo_ref, x_ref, mod_ref, o_ref):
    tm = x_ref.shape[0]
    gd = GMLP_WIDTH // GMLP_GROUPS
    chunks = []
    for c in range(tm // GMLP_CHUNK):
        rows = slice(c * GMLP_CHUNK, (c + 1) * GMLP_CHUNK)
        s = jnp.concatenate(
            [jnp.dot(sw_ref[g], vn_ref[rows, g * gd:(g + 1) * gd], preferred_element_type=F32)
             for g in range(GMLP_GROUPS)], axis=1) + sb_ref[...]
        chunks.append(gu_ref[rows, :].astype(F32) * s)
    sg = jnp.concatenate(chunks, axis=0).astype(BF16)
    y = (jnp.dot(att_ref[...], wo_ref[0:ATTN_WIDTH, :], preferred_element_type=F32)
         + jnp.dot(sg, wo_ref[ATTN_WIDTH:, :], preferred_element_type=F32))
    o_ref[...] = x_ref[...] + mod_ref[:, 2 * D_MODEL:3 * D_MODEL] * y


def _mix_out(att, gu, vn, sw, sb, w_out, h, mods, n_ctx_tiles):
    batch, t, _ = h.shape
    tm = TOKEN_TILE
    tok = lambda n: pl.BlockSpec((None, tm, n), lambda b, i: (b, i, 0))
    return pl.pallas_call(
        _mix_out_kernel,
        grid=(batch, t // tm),
        in_specs=[tok(ATTN_WIDTH), tok(GMLP_WIDTH), tok(GMLP_WIDTH),
                  _const_spec((GMLP_GROUPS, GMLP_CHUNK, GMLP_CHUNK)), _const_spec((GMLP_CHUNK, GMLP_WIDTH)),
                  _const_spec((D_MODEL, D_MODEL)), tok(D_MODEL), _mod_spec(n_ctx_tiles, batch)],
        out_specs=tok(D_MODEL),
        out_shape=jax.ShapeDtypeStruct(h.shape, F32),
        compiler_params=_cparams(2),
        name="mix_out",
    )(att, gu, vn, sw, sb, w_out, h, mods)


def _ffn_kernel(x_ref, mod_ref, g2_ref, wi_ref, wo_ref, o_ref, *, n_split):
    x = x_ref[...]
    mod = mod_ref[...]
    h = _rms_mod(x, g2_ref[...], mod[:, 3 * D_MODEL:4 * D_MODEL], mod[:, 4 * D_MODEL:5 * D_MODEL]).astype(BF16)
    hc = FFN_HIDDEN // n_split
    acc = jnp.zeros(x.shape, F32)
    for c in range(n_split):
        g = jnp.dot(h, wi_ref[:, c * hc:(c + 1) * hc], preferred_element_type=F32)
        u = jnp.dot(h, wi_ref[:, FFN_HIDDEN + c * hc:FFN_HIDDEN + (c + 1) * hc], preferred_element_type=F32)
        a = (_silu(g) * u).astype(BF16)
        acc = acc + jnp.dot(a, wo_ref[c * hc:(c + 1) * hc, :], preferred_element_type=F32)
    o_ref[...] = x + mod[:, 5 * D_MODEL:6 * D_MODEL] * acc


def _ffn(h, mods, g2, w_in, w_out, n_ctx_tiles):
    batch, t, _ = h.shape
    tm = TOKEN_TILE
    tok = pl.BlockSpec((None, tm, D_MODEL), lambda b, i: (b, i, 0))
    return pl.pallas_call(
        functools.partial(_ffn_kernel, n_split=2),
        grid=(batch, t // tm),
        in_specs=[tok, _mod_spec(n_ctx_tiles, batch), _const_spec((1, D_MODEL)),
                  _const_spec((D_MODEL, 2 * FFN_HIDDEN)), _const_spec((FFN_HIDDEN, D_MODEL))],
        out_specs=tok,
        out_shape=jax.ShapeDtypeStruct(h.shape, F32),
        compiler_params=_cparams(2),
        name="ffn",
    )(h, mods, g2, w_in, w_out)


def _ssm_in_kernel(xp_ref, x_ref, xn_ref, mod_ref, g1_ref, wz_ref, wx_ref, wdt_ref, cw_ref, cb_ref, dtb_ref,
                   z_ref, xs_ref, bm_ref, cm_ref, dt_ref, *, n_tiles, n_ctx_tiles, col_block):
    i = pl.program_id(1)
    tm = x_ref.shape[0]
    halo = SUBLANES
    prev_ok = jnp.logical_and(i != 0, i != n_ctx_tiles)
    next_ok = jnp.logical_and(i != n_ctx_tiles - 1, i != n_tiles - 1)
    xcat = jnp.concatenate([xp_ref[...], x_ref[...], xn_ref[...]], axis=0)
    mod = mod_ref[...]
    h = _rms_mod(xcat, g1_ref[...], mod[:, 0:D_MODEL], mod[:, D_MODEL:2 * D_MODEL]).astype(BF16)
    hm = h[halo:halo + tm, :]
    z_ref[...] = jnp.dot(hm, wz_ref[...], preferred_element_type=F32).astype(BF16)
    dtr = jnp.dot(hm, wdt_ref[...], preferred_element_type=F32) + dtb_ref[...]
    dt_ref[...] = jnp.maximum(dtr, 0.0) + jnp.log1p(jnp.exp(-jnp.abs(dtr)))
    row = lax.broadcasted_iota(jnp.int32, (tm + 2 * halo, 1), 0)
    keep = jnp.where(row < halo, prev_ok.astype(F32),
                     jnp.where(row >= tm + halo, next_ok.astype(F32), 1.0))
    for blk in range(SSM_CONV_DIM // col_block):
        cols = slice(blk * col_block, (blk + 1) * col_block)
        p = jnp.dot(h, wx_ref[:, cols], preferred_element_type=F32) * keep
        pm1 = pltpu.roll(p, 1, axis=0)[halo:halo + tm, :]
        pp1 = pltpu.roll(p, tm + 2 * halo - 1, axis=0)[halo:halo + tm, :]
        y = (cw_ref[0:1, cols] * pm1 + cw_ref[1:2, cols] * p[halo:halo + tm, :]
             + cw_ref[2:3, cols] * pp1 + cb_ref[:, cols])
        y = _silu(y).astype(BF16)
        c0 = blk * col_block
        if c0 < SSM_INNER:
            xs_ref[:, c0:c0 + col_block] = y
        elif c0 < SSM_INNER + SSM_BC_WIDTH:
            bm_ref[:, c0 - SSM_INNER:c0 - SSM_INNER + col_block] = y
        else:
            o0 = c0 - SSM_INNER - SSM_BC_WIDTH
            cm_ref[:, o0:o0 + col_block] = y


def _ssm_in(h, mods, g1, wz, wx, wdt, cw, cb, dtb, n_ctx_tiles):
    batch, t, _ = h.shape
    tm = TOKEN_TILE
    n_tiles = t // tm
    per = tm // SUBLANES
    tok = lambda n: pl.BlockSpec((None, tm, n), lambda b, i: (b, i, 0))
    kern = functools.partial(_ssm_in_kernel, n_tiles=n_tiles, n_ctx_tiles=n_ctx_tiles, col_block=512)
    return pl.pallas_call(
        kern,
        grid=(batch, n_tiles),
        in_specs=[pl.BlockSpec((None, SUBLANES, D_MODEL), lambda b, i: (b, jnp.maximum(i * per - 1, 0), 0)),
                  tok(D_MODEL),
                  pl.BlockSpec((None, SUBLANES, D_MODEL),
                               lambda b, i: (b, jnp.minimum((i + 1) * per, t // SUBLANES - 1), 0)),
                  _mod_spec(n_ctx_tiles, batch), _const_spec((1, D_MODEL)),
                  _const_spec((D_MODEL, SSM_INNER)), _const_spec((D_MODEL, SSM_CONV_DIM)),
                  _const_spec((D_MODEL, LANES)), _const_spec((SUBLANES, SSM_CONV_DIM)),
                  _const_spec((1, SSM_CONV_DIM)), _const_spec((1, LANES))],
        out_specs=[tok(SSM_INNER), tok(SSM_INNER), tok(SSM_BC_WIDTH), tok(SSM_BC_WIDTH), tok(LANES)],
        out_shape=[jax.ShapeDtypeStruct((batch, t, SSM_INNER), BF16),
                   jax.ShapeDtypeStruct((batch, t, SSM_INNER), BF16),
                   jax.ShapeDtypeStruct((batch, t, SSM_BC_WIDTH), BF16),
                   jax.ShapeDtypeStruct((batch, t, SSM_BC_WIDTH), BF16),
                   jax.ShapeDtypeStruct((batch, t, LANES), F32)],
        compiler_params=_cparams(2),
        name="ssm_in",
    )(h, h, h, mods, g1, wz, wx, wdt, cw, cb, dtb)


def _ssd_kernel(xs_ref, bm_ref, cm_ref, dt_ref, a_ref, ex_ref, y_ref, st_ref, *, reverse, lane_base):
    L = SSM_CHUNK

    @pl.when(pl.program_id(1) == 0)
    def _():
        st_ref[...] = jnp.zeros(st_ref.shape, F32)

    dt = dt_ref[...]
    da = dt * a_ref[...]
    r = lax.broadcasted_iota(jnp.int32, (L, L), 0)
    c = lax.broadcasted_iota(jnp.int32, (L, L), 1)
    before = (c >= r) if reverse else (c <= r)
    tri = jnp.where(before, 1.0, 0.0).astype(BF16)
    hi = da.astype(BF16)
    r1 = da - hi.astype(F32)
    mid = r1.astype(BF16)
    lo = (r1 - mid.astype(F32)).astype(BF16)
    p3 = jnp.dot(tri, jnp.concatenate([hi, mid, lo], axis=1), preferred_element_type=F32)
    P = p3[:, 0:LANES] + p3[:, LANES:2 * LANES] + p3[:, 2 * LANES:3 * LANES]
    tot = jnp.sum(da, axis=0, keepdims=True)
    PT = P.T
    dtT = dt.T
    w_state = dt * jnp.exp(tot - P)
    e_p = jnp.exp(P)

    def expand(v):
        vh, vl = _split2(v)
        return jnp.dot(jnp.concatenate([vh, vl], axis=1), ex_ref[...], preferred_element_type=F32)

    w_state_x = expand(w_state)
    e_p_x = expand(e_p)
    end = 0 if reverse else L - 1
    lane = lax.broadcasted_iota(jnp.int32, (L, LANES), 1)
    heads_per_group = SSM_HEADS // SSM_GROUPS
    gw = heads_per_group * SSM_HEAD_DIM
    for g in range(SSM_GROUPS):
        bg = bm_ref[:, g * SSM_STATE:(g + 1) * SSM_STATE]
        cg = cm_ref[:, g * SSM_STATE:(g + 1) * SSM_STATE]
        cb = lax.dot_general(cg, bg, (((1,), (1,)), ((), ())), preferred_element_type=F32)
        st = st_ref[g]
        y_off = jnp.dot(cg, st.astype(BF16), preferred_element_type=F32) * e_p_x[:, g * gw:(g + 1) * gw]
        ys = []
        for pair in range(heads_per_group // 2):
            ms = []
            for k in range(2):
                hl = lane_base + g * heads_per_group + 2 * pair + k
                seg = P[:, hl:hl + 1] - PT[hl:hl + 1, :]
                dec = jnp.exp(jnp.where(before, seg, -jnp.inf))
                ms.append((cb * dec * dtT[hl:hl + 1, :]).astype(BF16))
            xp = xs_ref[:, g * gw + pair * LANES:g * gw + (pair + 1) * LANES].astype(F32)
            rhs = jnp.concatenate([jnp.where(lane < SSM_HEAD_DIM, xp, 0.0),
                                   jnp.where(lane >= SSM_HEAD_DIM, xp, 0.0)], axis=0).astype(BF16)
            ys.append(jnp.dot(jnp.concatenate(ms, axis=1), rhs, preferred_element_type=F32))
        y_ref[:, g * gw:(g + 1) * gw] = (jnp.concatenate(ys, axis=1) + y_off).astype(BF16)
        xdd = (xs_ref[:, g * gw:(g + 1) * gw].astype(F32) * w_state_x[:, g * gw:(g + 1) * gw]).astype(BF16)
        bgT = bg.astype(F32).T.astype(BF16)
        st_ref[g] = (st * e_p_x[end:end + 1, g * gw:(g + 1) * gw]
                     + jnp.dot(bgT, xdd, preferred_element_type=F32))


def _ssd(xs, bm, cm, dt, a_row, ex, reverse, n_ctx_chunks):
    batch, t, _ = xs.shape
    L = SSM_CHUNK
    nc = t // L
    if reverse:
        cmap = lambda i: jnp.where(i < n_ctx_chunks, n_ctx_chunks - 1 - i, nc - 1 + n_ctx_chunks - i)
    else:
        cmap = lambda i: i
    tok = lambda n: pl.BlockSpec((None, L, n), lambda b, i: (b, cmap(i), 0))
    kern = functools.partial(_ssd_kernel, reverse=reverse, lane_base=SSM_HEADS if reverse else 0)
    return pl.pallas_call(
        kern,
        grid=(batch, nc),
        in_specs=[tok(SSM_INNER), tok(SSM_BC_WIDTH), tok(SSM_BC_WIDTH), tok(LANES),
                  _const_spec((1, LANES)), _const_spec((2 * LANES, SSM_INNER))],
        out_specs=tok(SSM_INNER),
        out_shape=jax.ShapeDtypeStruct((batch, t, SSM_INNER), BF16),
        scratch_shapes=[pltpu.VMEM((SSM_GROUPS, SSM_STATE, SSM_INNER // SSM_GROUPS), F32)],
        compiler_params=_cparams(2),
        name="ssd_bwd" if reverse else "ssd_fwd",
    )(xs, bm, cm, dt, a_row, ex)


def _ssm_out_kernel(yf_ref, yb_ref, xs_ref, z_ref, dsk_ref, ng_ref, wo_ref, x_ref, mod_ref, o_ref):
    y = yf_ref[...].astype(F32) + yb_ref[...].astype(F32) + dsk_ref[...] * xs_ref[...].astype(F32)
    y = y * _silu(z_ref[...].astype(F32))
    gw = SSM_INNER // SSM_GROUPS
    parts = []
    for g in range(SSM_GROUPS):
        blk = y[:, g * gw:(g + 1) * gw]
        ms = jnp.mean(blk * blk, axis=-1, keepdims=True)
        parts.append(blk * lax.rsqrt(ms + NORM_EPS))
    yn = (jnp.concatenate(parts, axis=1) * ng_ref[...]).astype(BF16)
    out = jnp.dot(yn, wo_ref[...], preferred_element_type=F32)
    o_ref[...] = x_ref[...] + mod_ref[:, 2 * D_MODEL:3 * D_MODEL] * out


def _ssm_out(yf, yb, xs, z, dsk, ng, w_out, h, mods, n_ctx_tiles):
    batch, t, _ = h.shape
    tm = TOKEN_TILE
    tok = lambda n: pl.BlockSpec((None, tm, n), lambda b, i: (b, i, 0))
    return pl.pallas_call(
        _ssm_out_kernel,
        grid=(batch, t // tm),
        in_specs=[tok(SSM_INNER), tok(SSM_INNER), tok(SSM_INNER), tok(SSM_INNER),
                  _const_spec((1, SSM_INNER)), _const_spec((1, SSM_INNER)),
                  _const_spec((SSM_INNER, D_MODEL)), tok(D_MODEL), _mod_spec(n_ctx_tiles, batch)],
        out_specs=tok(D_MODEL),
        out_shape=jax.ShapeDtypeStruct(h.shape, F32),
        compiler_params=_cparams(2),
        name="ssm_out",
    )(yf, yb, xs, z, dsk, ng, w_out, h, mods)


def _rope_tables(ctx_len, seq):
    tpos = jnp.arange(seq)
    row = (tpos // GRID_W).astype(F32)
    col = (tpos % GRID_W).astype(F32)
    n_freq = HEAD_DIM // 4
    inv = ROPE_THETA ** (-jnp.arange(n_freq, dtype=F32) / n_freq)
    ang = jnp.concatenate([row[:, None] * inv, col[:, None] * inv], axis=-1)
    cos = jnp.cos(ang)
    sin = jnp.sin(ang)
    cos64 = jnp.concatenate([cos, cos], axis=-1)
    sin64 = jnp.concatenate([-sin, sin], axis=-1)
    cos_t = jnp.concatenate([jnp.ones((ctx_len, HEAD_DIM), F32), cos64], axis=0)
    sin_t = jnp.concatenate([jnp.zeros((ctx_len, HEAD_DIM), F32), sin64], axis=0)
    return jnp.tile(cos_t, (1, N_KV_HEADS)), jnp.tile(sin_t, (1, N_KV_HEADS))


def _block_mean_matrix(n, group):
    idx = np.arange(n) // group
    return jnp.asarray((idx[:, None] == idx[None, :]).astype(np.float32) / group, dtype=BF16)


def kernel(x, c, ctx, c_ctx, ada_w, ada_b, norm1_g, norm2_g, ffn_w_in, ffn_w_out, mix_w_in, mix_w_out,
           q_norm_g, k_norm_g, sgu_norm_g, sgu_w, sgu_b, ssm_w_in, ssm_conv_w, ssm_conv_b, ssm_dt_bias,
           ssm_a_log, ssm_d, ssm_norm_g, ssm_w_out):
    batch, seq, _ = x.shape
    ctx_len = ctx.shape[1]
    depth = ada_w.shape[0]
    assert ctx_len % TOKEN_TILE == 0 and seq % TOKEN_TILE == 0 and batch < SUBLANES
    n_ctx_tiles = ctx_len // TOKEN_TILE
    n_ctx_chunks = ctx_len // SSM_CHUNK

    h = jnp.concatenate([ctx, x], axis=1)
    cond = jnp.zeros((SUBLANES, D_MODEL), F32).at[:batch].set(c).at[batch].set(c_ctx)
    mods = _adaln_all(cond, ada_w, ada_b)[:, :, None, :]

    cos_t, sin_t = _rope_tables(ctx_len, seq)
    gq = _block_mean_matrix(ATTN_WIDTH, HEAD_DIM)
    gk = _block_mean_matrix(KV_WIDTH, HEAD_DIM)
    perm = np.arange(ATTN_WIDTH).reshape(N_KV_HEADS, Q_PER_KV, HEAD_DIM).transpose(1, 0, 2).reshape(-1)
    ex_np = np.zeros((2, 2 * LANES, SSM_INNER), np.float32)
    for d in range(2):
        for hd in range(SSM_HEADS):
            for half in range(2):
                ex_np[d, half * LANES + d * SSM_HEADS + hd, hd * SSM_HEAD_DIM:(hd + 1) * SSM_HEAD_DIM] = 1.0
    ex = jnp.asarray(ex_np, dtype=BF16)

    for i in range(depth):
        j = i // 2
        m = mods[i]
        g1 = norm1_g[i][None, :]
        if i % 2 == 0:
            qT, k, vT, gu, vn = _mix_in(
                h, m, g1, mix_w_in[j].astype(BF16), jnp.tile(q_norm_g[j], N_Q_HEADS)[None, :],
                jnp.tile(k_norm_g[j], N_KV_HEADS)[None, :], cos_t, sin_t, gq, gk,
                sgu_norm_g[j][None, :], n_ctx_tiles)
            att = _attention(qT, k, vT, n_ctx_tiles, ctx_len)
            w_out = mix_w_out[j]
            w_out = jnp.concatenate([w_out[:ATTN_WIDTH][perm], w_out[ATTN_WIDTH:]], axis=0).astype(BF16)
            sb = jnp.repeat(sgu_b[j].T, GMLP_WIDTH // GMLP_GROUPS, axis=1)
            h = _mix_out(att, gu, vn, sgu_w[j].astype(BF16), sb, w_out, h, m, n_ctx_tiles)
        else:
            w_in = ssm_w_in[j]
            wz = w_in[:, :SSM_INNER].astype(BF16)
            wx = w_in[:, SSM_INNER:SSM_INNER + SSM_CONV_DIM].astype(BF16)
            n_dt = 2 * SSM_HEADS
            wdt = jnp.zeros((D_MODEL, LANES), F32).at[:, :n_dt].set(w_in[:, SSM_INNER + SSM_CONV_DIM:]).astype(BF16)
            dtb = jnp.zeros((1, LANES), F32).at[0, :n_dt].set(ssm_dt_bias[j].reshape(-1))
            cw = jnp.zeros((SUBLANES, SSM_CONV_DIM), F32).at[:ssm_conv_w.shape[1]].set(ssm_conv_w[j])
            z, xs, bm, cm, dt = _ssm_in(h, m, g1, wz, wx, wdt, cw, ssm_conv_b[j][None, :], dtb, n_ctx_tiles)
            a = -jnp.exp(ssm_a_log[j].astype(F32))
            ys = []
            for d in range(2):
                a_row = jnp.zeros((1, LANES), F32).at[0, d * SSM_HEADS:(d + 1) * SSM_HEADS].set(a[d])
                ys.append(_ssd(xs, bm, cm, dt, a_row, ex[d], d == 1, n_ctx_chunks))
            dsk = jnp.repeat(ssm_d[j], SSM_HEAD_DIM)[None, :]
            h = _ssm_out(ys[0], ys[1], xs, z, dsk, ssm_norm_g[j][None, :], ssm_w_out[j].astype(BF16), h, m,
                         n_ctx_tiles)
        h = _ffn(h, m, norm2_g[i][None, :], ffn_w_in[i].astype(BF16), ffn_w_out[i].astype(BF16), n_ctx_tiles)
    return h[:, ctx_len:, :]
```

```python
import functools
import math

import jax
import jax.numpy as jnp
import numpy as np
from jax import lax
from jax.experimental import pallas as pl
from jax.experimental.pallas import tpu as pltpu

F32 = jnp.float32
BF16 = jnp.bfloat16

D_MODEL = 1024
GRID_W = 64
HEAD_DIM = 64
ATTN_WIDTH = D_MODEL // 2
N_Q_HEADS = ATTN_WIDTH // HEAD_DIM
N_KV_HEADS = 2
Q_PER_KV = N_Q_HEADS // N_KV_HEADS
KV_WIDTH = N_KV_HEADS * HEAD_DIM
ROPE_THETA = 10000.0
GMLP_WIDTH = D_MODEL // 2
GMLP_CHUNK = 128
GMLP_GROUPS = 4
MIX_IN = ATTN_WIDTH + 2 * KV_WIDTH + 2 * GMLP_WIDTH
SSM_INNER = 2 * D_MODEL
SSM_HEAD_DIM = 64
SSM_HEADS = SSM_INNER // SSM_HEAD_DIM
SSM_GROUPS = 4
SSM_STATE = 128
SSM_CHUNK = 128
SSM_BC_WIDTH = SSM_GROUPS * SSM_STATE
SSM_CONV_DIM = SSM_INNER + 2 * SSM_BC_WIDTH
FFN_HIDDEN = -(-8 * D_MODEL // (3 * 256)) * 256
NORM_EPS = 1e-6

LANES = 128
SUBLANES = 8
TOKEN_TILE = 256
VMEM_LIMIT = 56 * 1024 * 1024
LOG2E = 1.4426950408889634
BF16_ROWS = 16
V_ROWS = HEAD_DIM + BF16_ROWS
SAFE_LOG2_BOUND = 60.0


def _cparams(n_axes):
    return pltpu.CompilerParams(dimension_semantics=("arbitrary",) * n_axes,
                                vmem_limit_bytes=VMEM_LIMIT)


def _silu(x):
    return x / (1.0 + jnp.exp(-x))


def _gelu_tanh(x):
    c = math.sqrt(2.0 / math.pi)
    return x * (0.5 * (1.0 + jnp.tanh(c * (x + 0.044715 * (x * x * x)))))


def _rms_mod(x, g, shift, scale):
    ms = jnp.mean(x * x, axis=-1, keepdims=True)
    y = (x * lax.rsqrt(ms + NORM_EPS)) * g
    return y * (1.0 + scale) + shift


def _split2(x):
    hi = x.astype(BF16)
    lo = (x - hi.astype(F32)).astype(BF16)
    return hi, lo


def _group_mean(x2, gmat):
    hi, lo = _split2(x2)
    return (jnp.dot(hi, gmat, preferred_element_type=F32)
            + jnp.dot(lo, gmat, preferred_element_type=F32))


def _rope(x, cos, sin_signed):
    n = x.shape[-1]
    half = HEAD_DIM // 2
    fwd = pltpu.roll(x, n - half, axis=1)
    bwd = pltpu.roll(x, half, axis=1)
    lane = lax.broadcasted_iota(jnp.int32, x.shape, 1)
    partner = jnp.where((lane % HEAD_DIM) < half, fwd, bwd)
    return x * cos + partner * sin_signed


def _adaln_kernel(c_ref, w_ref, b_ref, o_ref):
    s = _silu(c_ref[...]).astype(BF16)
    o_ref[...] = jnp.dot(s, w_ref[...].astype(BF16), preferred_element_type=F32) + b_ref[...]


def _adaln_all(cond, ada_w, ada_b):
    depth = ada_w.shape[0]
    tn = 1536
    return pl.pallas_call(
        _adaln_kernel,
        grid=(depth, 6 * D_MODEL // tn),
        in_specs=[pl.BlockSpec((SUBLANES, D_MODEL), lambda l, j: (0, 0)),
                  pl.BlockSpec((None, D_MODEL, tn), lambda l, j: (l, 0, j)),
                  pl.BlockSpec((None, 1, tn), lambda l, j: (l, 0, j))],
        out_specs=pl.BlockSpec((None, SUBLANES, tn), lambda l, j: (l, 0, j)),
        out_shape=jax.ShapeDtypeStruct((depth, SUBLANES, 6 * D_MODEL), F32),
        compiler_params=_cparams(2),
        name="adaln",
    )(cond, ada_w, ada_b.reshape(depth, 1, 6 * D_MODEL))


def _mod_spec(n_ctx_tiles, batch):
    return pl.BlockSpec((None, 1, 6 * D_MODEL),
                        lambda b, i: (jnp.where(i < n_ctx_tiles, batch, b), 0, 0))


def _const_spec(shape):
    nd = len(shape)
    return pl.BlockSpec(shape, lambda b, i: (0,) * nd)


def _mix_in_kernel(x_ref, mod_ref, g1_ref, w_ref, qg_ref, kg_ref, cos_ref, sin_ref, gq_ref, gk_ref,
                   sg_ref, qT_ref, k_ref, vT_ref, gu_ref, vn_ref):
    mod = mod_ref[...]
    h = _rms_mod(x_ref[...], g1_ref[...], mod[:, 0:D_MODEL], mod[:, D_MODEL:2 * D_MODEL]).astype(BF16)
    p = jnp.dot(h, w_ref[...], preferred_element_type=F32)
    q = p[:, 0:ATTN_WIDTH]
    k = p[:, ATTN_WIDTH:ATTN_WIDTH + KV_WIDTH]
    v = p[:, ATTN_WIDTH + KV_WIDTH:ATTN_WIDTH + 2 * KV_WIDTH]
    gm = p[:, ATTN_WIDTH + 2 * KV_WIDTH:]
    cos = cos_ref[...]
    sin = sin_ref[...]
    q = q * lax.rsqrt(_group_mean(q * q, gq_ref[...]) + NORM_EPS) * qg_ref[...]
    k = k * lax.rsqrt(_group_mean(k * k, gk_ref[...]) + NORM_EPS) * kg_ref[...]
    reps = ATTN_WIDTH // KV_WIDTH
    q = _rope(q, jnp.concatenate([cos] * reps, axis=1), jnp.concatenate([sin] * reps, axis=1))
    k = _rope(k, cos, sin)
    q = q * (HEAD_DIM ** -0.5 * LOG2E)
    qT_ref[...] = q.T.astype(BF16)
    k_ref[...] = k.astype(BF16)
    vT = v.T
    ones = jnp.ones((BF16_ROWS, vT.shape[1]), F32)
    vT_ref[...] = jnp.concatenate(
        [blk for kvh in range(N_KV_HEADS) for blk in (vT[kvh * HEAD_DIM:(kvh + 1) * HEAD_DIM, :], ones)],
        axis=0).astype(BF16)
    ge = _gelu_tanh(gm)
    gu_ref[...] = ge[:, 0:GMLP_WIDTH].astype(BF16)
    gv = ge[:, GMLP_WIDTH:]
    gd = GMLP_WIDTH // GMLP_GROUPS
    parts = []
    for g in range(GMLP_GROUPS):
        blk = gv[:, g * gd:(g + 1) * gd]
        ms = jnp.mean(blk * blk, axis=-1, keepdims=True)
        parts.append(blk * lax.rsqrt(ms + NORM_EPS))
    vn_ref[...] = (jnp.concatenate(parts, axis=1) * sg_ref[...]).astype(BF16)


def _mix_in(h, mods, g1, w_in, qg, kg, cos, sin, gq, gk, sgg, n_ctx_tiles):
    batch, t, _ = h.shape
    tm = TOKEN_TILE
    tok = lambda n: pl.BlockSpec((None, tm, n), lambda b, i: (b, i, 0))
    tr = lambda n: pl.BlockSpec((None, n, tm), lambda b, i: (b, 0, i))
    return pl.pallas_call(
        _mix_in_kernel,
        grid=(batch, t // tm),
        in_specs=[tok(D_MODEL), _mod_spec(n_ctx_tiles, batch), _const_spec((1, D_MODEL)),
                  _const_spec((D_MODEL, MIX_IN)), _const_spec((1, ATTN_WIDTH)), _const_spec((1, KV_WIDTH)),
                  pl.BlockSpec((tm, KV_WIDTH), lambda b, i: (i, 0)),
                  pl.BlockSpec((tm, KV_WIDTH), lambda b, i: (i, 0)),
                  _const_spec((ATTN_WIDTH, ATTN_WIDTH)), _const_spec((KV_WIDTH, KV_WIDTH)),
                  _const_spec((1, GMLP_WIDTH))],
        out_specs=[tr(ATTN_WIDTH), tok(KV_WIDTH), tr(N_KV_HEADS * V_ROWS), tok(GMLP_WIDTH), tok(GMLP_WIDTH)],
        out_shape=[jax.ShapeDtypeStruct((batch, ATTN_WIDTH, t), BF16),
                   jax.ShapeDtypeStruct((batch, t, KV_WIDTH), BF16),
                   jax.ShapeDtypeStruct((batch, N_KV_HEADS * V_ROWS, t), BF16),
                   jax.ShapeDtypeStruct((batch, t, GMLP_WIDTH), BF16),
                   jax.ShapeDtypeStruct((batch, t, GMLP_WIDTH), BF16)],
        compiler_params=_cparams(2),
        name="mix_in",
    )(h, mods, g1, w_in, qg, kg, cos, sin, gq, gk, sgg)


def _attn_kernel(safe_ref, qT_ref, k_ref, vT_ref, o_ref, qpad_ref, acc_ref, m_ref, *, tq, tk, tk_main,
                 n_ctx_tiles, ctx_len, t):
    i = pl.program_id(1)
    for kvh in range(N_KV_HEADS):
        qcat = jnp.concatenate(
            [qT_ref[(kvh * Q_PER_KV + g) * HEAD_DIM:(kvh * Q_PER_KV + g + 1) * HEAD_DIM, :]
             for g in range(Q_PER_KV)], axis=1)
        z = jnp.zeros_like(qcat)
        qpad_ref[kvh] = jnp.concatenate([qcat, z] if kvh == 0 else [z, qcat], axis=0)
    acc_ref[...] = jnp.zeros(acc_ref.shape, F32)
    is_latent = i >= n_ctx_tiles
    safe = safe_ref[0] != 0

    def plain_step(off, width):
        kblk = k_ref[pl.ds(off, width), :]
        for kvh in range(N_KV_HEADS):
            s = jnp.dot(kblk, qpad_ref[kvh], preferred_element_type=F32)
            p = jnp.exp2(s).astype(BF16)
            vblk = vT_ref[kvh * V_ROWS:(kvh + 1) * V_ROWS, pl.ds(off, width)]
            acc_ref[kvh] += jnp.dot(vblk, p, preferred_element_type=F32)

    @pl.when(safe)
    def _():
        plain_step(0, ctx_len)

        @pl.when(is_latent)
        def _():
            def body(j, carry):
                plain_step(pl.multiple_of(ctx_len + j * tk_main, LANES), tk_main)
                return carry
            lax.fori_loop(0, (t - ctx_len) // tk_main, body, 0)

    @pl.when(jnp.logical_not(safe))
    def _():
        m_ref[...] = jnp.full(m_ref.shape, -jnp.inf, F32)

        def body(j, carry):
            off = pl.multiple_of(j * tk, tk)
            kblk = k_ref[pl.ds(off, tk), :]
            for kvh in range(N_KV_HEADS):
                s = jnp.dot(kblk, qpad_ref[kvh], preferred_element_type=F32)
                m_old = m_ref[kvh]
                m_new = jnp.maximum(m_old, jnp.max(s, axis=0, keepdims=True))
                alpha = jnp.exp2(m_old - m_new)
                p = jnp.exp2(s - m_new).astype(BF16)
                vblk = vT_ref[kvh * V_ROWS:(kvh + 1) * V_ROWS, pl.ds(off, tk)]
                acc_ref[kvh] = acc_ref[kvh] * alpha + jnp.dot(vblk, p, preferred_element_type=F32)
                m_ref[kvh] = m_new
            return carry

        lax.fori_loop(0, jnp.where(is_latent, t // tk, ctx_len // tk), body, 0)

    for g in range(Q_PER_KV):
        og = jnp.concatenate(
            [acc_ref[kvh, 0:HEAD_DIM, g * tq:(g + 1) * tq]
             / acc_ref[kvh, HEAD_DIM:HEAD_DIM + 1, g * tq:(g + 1) * tq]
             for kvh in range(N_KV_HEADS)], axis=0)
        o_ref[:, g * KV_WIDTH:(g + 1) * KV_WIDTH] = og.T.astype(BF16)


def _attention(safe, qT, k, vT, n_ctx_tiles, ctx_len):
    batch, _, t = qT.shape
    tq = TOKEN_TILE
    tk = 256
    tk_main = 512
    assert ctx_len % tk == 0 and t % tk == 0 and (t - ctx_len) % tk_main == 0
    kern = functools.partial(_attn_kernel, tq=tq, tk=tk, tk_main=tk_main, n_ctx_tiles=n_ctx_tiles,
                             ctx_len=ctx_len, t=t)
    return pl.pallas_call(
        kern,
        grid=(batch, t // tq),
        in_specs=[pl.BlockSpec(memory_space=pltpu.SMEM),
                  pl.BlockSpec((None, ATTN_WIDTH, tq), lambda b, i: (b, 0, i)),
                  pl.BlockSpec((None, t, KV_WIDTH), lambda b, i: (b, 0, 0)),
                  pl.BlockSpec((None, N_KV_HEADS * V_ROWS, t), lambda b, i: (b, 0, 0))],
        out_specs=pl.BlockSpec((None, tq, ATTN_WIDTH), lambda b, i: (b, i, 0)),
        out_shape=jax.ShapeDtypeStruct((batch, t, ATTN_WIDTH), BF16),
        scratch_shapes=[pltpu.VMEM((N_KV_HEADS, KV_WIDTH, Q_PER_KV * tq), BF16),
                        pltpu.VMEM((N_KV_HEADS, V_ROWS, Q_PER_KV * tq), F32),
                        pltpu.VMEM((N_KV_HEADS, 1, Q_PER_KV * tq), F32)],
        compiler_params=_cparams(2),
        name="attention",
    )(safe, qT, k, vT)


def _mix_out_kernel(att_ref, gu_ref, vn_ref, sw_ref, sb_ref, wo_ref, x_ref, mod_ref, o_ref):
    tm = x_ref.shape[0]
    gd = GMLP_WIDTH // GMLP_GROUPS
    chunks = []
    for c in range(tm // GMLP_CHUNK):
        rows = slice(c * GMLP_CHUNK, (c + 1) * GMLP_CHUNK)
        s = jnp.concatenate(
            [jnp.dot(sw_ref[g], vn_ref[rows, g * gd:(g + 1) * gd], preferred_element_type=F32)
             for g in range(GMLP_GROUPS)], axis=1) + sb_ref[...]
        chunks.append(gu_ref[rows, :].astype(F32) * s)
    sg = jnp.concatenate(chunks, axis=0).astype(BF16)
    y = (jnp.dot(att_ref[...], wo_ref[0:ATTN_WIDTH, :], preferred_element_type=F32)
         + jnp.dot(sg, wo_ref[ATTN_WIDTH:, :], preferred_element_type=F32))
    o_ref[...] = x_ref[...] + mod_ref[:, 2 * D_MODEL:3 * D_MODEL] * y


def _mix_out(att, gu, vn, sw, sb, w_out, h, mods, n_ctx_tiles):
    batch, t, _ = h.shape
    tm = TOKEN_TILE
    tok = lambda n: pl.BlockSpec((None, tm, n), lambda b, i: (b, i, 0))
    return pl.pallas_call(
        _mix_out_kernel,
        grid=(batch, t // tm),
        in_specs=[tok(ATTN_WIDTH), tok(GMLP_WIDTH), tok(GMLP_WIDTH),
                  _const_spec((GMLP_GROUPS, GMLP_CHUNK, GMLP_CHUNK)), _const_spec((GMLP_CHUNK, GMLP_WIDTH)),
                  _const_spec((D_MODEL, D_MODEL)), tok(D_MODEL), _mod_spec(n_ctx_tiles, batch)],
        out_specs=tok(D_MODEL),
        out_shape=jax.ShapeDtypeStruct(h.shape, F32),
        compiler_params=_cparams(2),
        name="mix_out",
    )(att, gu, vn, sw, sb, w_out, h, mods)


def _ffn_kernel(x_ref, mod_ref, g2_ref, wi_ref, wo_ref, o_ref, *, n_split):
    x = x_ref[...]
    mod = mod_ref[...]
    h = _rms_mod(x, g2_ref[...], mod[:, 3 * D_MODEL:4 * D_MODEL], mod[:, 4 * D_MODEL:5 * D_MODEL]).astype(BF16)
    hc = FFN_HIDDEN // n_split
    acc = jnp.zeros(x.shape, F32)
    for c in range(n_split):
        g = jnp.dot(h, wi_ref[:, c * hc:(c + 1) * hc], preferred_element_type=F32)
        u = jnp.dot(h, wi_ref[:, FFN_HIDDEN + c * hc:FFN_HIDDEN + (c + 1) * hc], preferred_element_type=F32)
        a = (_silu(g) * u).astype(BF16)
        acc = acc + jnp.dot(a, wo_ref[c * hc:(c + 1) * hc, :], preferred_element_type=F32)
    o_ref[...] = x + mod[:, 5 * D_MODEL:6 * D_MODEL] * acc


def _ffn(h, mods, g2, w_in, w_out, n_ctx_tiles):
    batch, t, _ = h.shape
    tm = TOKEN_TILE
    tok = pl.BlockSpec((None, tm, D_MODEL), lambda b, i: (b, i, 0))
    return pl.pallas_call(
        functools.partial(_ffn_kernel, n_split=2),
        grid=(batch, t // tm),
        in_specs=[tok, _mod_spec(n_ctx_tiles, batch), _const_spec((1, D_MODEL)),
                  _const_spec((D_MODEL, 2 * FFN_HIDDEN)), _const_spec((FFN_HIDDEN, D_MODEL))],
        out_specs=tok,
        out_shape=jax.ShapeDtypeStruct(h.shape, F32),
        compiler_params=_cparams(2),
        name="ffn",
    )(h, mods, g2, w_in, w_out)


def _ssm_in_kernel(xp_ref, x_ref, xn_ref, mod_ref, g1_ref, wz_ref, wx_ref, wdt_ref, cw_ref, cb_ref, dtb_ref,
                   z_ref, xs_ref, bm_ref, cm_ref, dt_ref, *, n_tiles, n_ctx_tiles, col_block):
    i = pl.program_id(1)
    tm = x_ref.shape[0]
    halo = SUBLANES
    prev_ok = jnp.logical_and(i != 0, i != n_ctx_tiles)
    next_ok = jnp.logical_and(i != n_ctx_tiles - 1, i != n_tiles - 1)
    xcat = jnp.concatenate([xp_ref[...], x_ref[...], xn_ref[...]], axis=0)
    mod = mod_ref[...]
    h = _rms_mod(xcat, g1_ref[...], mod[:, 0:D_MODEL], mod[:, D_MODEL:2 * D_MODEL]).astype(BF16)
    hm = h[halo:halo + tm, :]
    z_ref[...] = jnp.dot(hm, wz_ref[...], preferred_element_type=F32).astype(BF16)
    dtr = jnp.dot(hm, wdt_ref[...], preferred_element_type=F32) + dtb_ref[...]
    dt_ref[...] = jnp.maximum(dtr, 0.0) + jnp.log1p(jnp.exp(-jnp.abs(dtr)))
    row = lax.broadcasted_iota(jnp.int32, (tm + 2 * halo, 1), 0)
    keep = jnp.where(row < halo, prev_ok.astype(F32),
                     jnp.where(row >= tm + halo, next_ok.astype(F32), 1.0))
    for blk in range(SSM_CONV_DIM // col_block):
        cols = slice(blk * col_block, (blk + 1) * col_block)
        p = jnp.dot(h, wx_ref[:, cols], preferred_element_type=F32) * keep
        pm1 = pltpu.roll(p, 1, axis=0)[halo:halo + tm, :]
        pp1 = pltpu.roll(p, tm + 2 * halo - 1, axis=0)[halo:halo + tm, :]
        y = (cw_ref[0:1, cols] * pm1 + cw_ref[1:2, cols] * p[halo:halo + tm, :]
             + cw_ref[2:3, cols] * pp1 + cb_ref[:, cols])
        y = _silu(y).astype(BF16)
        c0 = blk * col_block
        if c0 < SSM_INNER:
            xs_ref[:, c0:c0 + col_block] = y
        elif c0 < SSM_INNER + SSM_BC_WIDTH:
            bm_ref[:, c0 - SSM_INNER:c0 - SSM_INNER + col_block] = y
        else:
            o0 = c0 - SSM_INNER - SSM_BC_WIDTH
            cm_ref[:, o0:o0 + col_block] = y


def _ssm_in(h, mods, g1, wz, wx, wdt, cw, cb, dtb, n_ctx_tiles):
    batch, t, _ = h.shape
    tm = TOKEN_TILE
    n_tiles = t // tm
    per = tm // SUBLANES
    tok = lambda n: pl.BlockSpec((None, tm, n), lambda b, i: (b, i, 0))
    kern = functools.partial(_ssm_in_kernel, n_tiles=n_tiles, n_ctx_tiles=n_ctx_tiles, col_block=512)
    return pl.pallas_call(
        kern,
        grid=(batch, n_tiles),
        in_specs=[pl.BlockSpec((None, SUBLANES, D_MODEL), lambda b, i: (b, jnp.maximum(i * per - 1, 0), 0)),
                  tok(D_MODEL),
                  pl.BlockSpec((None, SUBLANES, D_MODEL),
                               lambda b, i: (b, jnp.minimum((i + 1) * per, t // SUBLANES - 1), 0)),
                  _mod_spec(n_ctx_tiles, batch), _const_spec((1, D_MODEL)),
                  _const_spec((D_MODEL, SSM_INNER)), _const_spec((D_MODEL, SSM_CONV_DIM)),
                  _const_spec((D_MODEL, LANES)), _const_spec((SUBLANES, SSM_CONV_DIM)),
                  _const_spec((1, SSM_CONV_DIM)), _const_spec((1, LANES))],
        out_specs=[tok(SSM_INNER), tok(SSM_INNER), tok(SSM_BC_WIDTH), tok(SSM_BC_WIDTH), tok(LANES)],
        out_shape=[jax.ShapeDtypeStruct((batch, t, SSM_INNER), BF16),
                   jax.ShapeDtypeStruct((batch, t, SSM_INNER), BF16),
                   jax.ShapeDtypeStruct((batch, t, SSM_BC_WIDTH), BF16),
                   jax.ShapeDtypeStruct((batch, t, SSM_BC_WIDTH), BF16),
                   jax.ShapeDtypeStruct((batch, t, LANES), F32)],
        compiler_params=_cparams(2),
        name="ssm_in",
    )(h, h, h, mods, g1, wz, wx, wdt, cw, cb, dtb)


def _ssd_kernel(xs_ref, bm_ref, cm_ref, dt_ref, a_ref, ex_ref, y_ref, st_ref, *, reverse, lane_base):
    L = SSM_CHUNK

    @pl.when(pl.program_id(1) == 0)
    def _():
        st_ref[...] = jnp.zeros(st_ref.shape, F32)

    dt = dt_ref[...]
    da = dt * a_ref[...]
    r = lax.broadcasted_iota(jnp.int32, (L, L), 0)
    c = lax.broadcasted_iota(jnp.int32, (L, L), 1)
    before = (c >= r) if reverse else (c <= r)
    tri = jnp.where(before, 1.0, 0.0).astype(BF16)
    hi = da.astype(BF16)
    r1 = da - hi.astype(F32)
    mid = r1.astype(BF16)
    lo = (r1 - mid.astype(F32)).astype(BF16)
    p3 = jnp.dot(tri, jnp.concatenate([hi, mid, lo], axis=1), preferred_element_type=F32)
    P = p3[:, 0:LANES] + p3[:, LANES:2 * LANES] + p3[:, 2 * LANES:3 * LANES]
    tot = jnp.sum(da, axis=0, keepdims=True)
    PT = P.T
    dtT = dt.T
    w_state = dt * jnp.exp(tot - P)
    e_p = jnp.exp(P)

    def expand(v):
        vh, vl = _split2(v)
        return jnp.dot(jnp.concatenate([vh, vl], axis=1), ex_ref[...], preferred_element_type=F32)

    w_state_x = expand(w_state)
    e_p_x = expand(e_p)
    end = 0 if reverse else L - 1
    lane = lax.broadcasted_iota(jnp.int32, (L, LANES), 1)
    heads_per_group = SSM_HEADS // SSM_GROUPS
    gw = heads_per_group * SSM_HEAD_DIM
    for g in range(SSM_GROUPS):
        bg = bm_ref[:, g * SSM_STATE:(g + 1) * SSM_STATE]
        cg = cm_ref[:, g * SSM_STATE:(g + 1) * SSM_STATE]
        cb = lax.dot_general(cg, bg, (((1,), (1,)), ((), ())), preferred_element_type=F32)
        st = st_ref[g]
        y_off = jnp.dot(cg, st.astype(BF16), preferred_element_type=F32) * e_p_x[:, g * gw:(g + 1) * gw]
        ys = []
        for pair in range(heads_per_group // 2):
            ms = []
            for k in range(2):
                hl = lane_base + g * heads_per_group + 2 * pair + k
                seg = P[:, hl:hl + 1] - PT[hl:hl + 1, :]
                dec = jnp.exp(jnp.where(before, seg, -jnp.inf))
                ms.append((cb * dec * dtT[hl:hl + 1, :]).astype(BF16))
            xp = xs_ref[:, g * gw + pair * LANES:g * gw + (pair + 1) * LANES].astype(F32)
            rhs = jnp.concatenate([jnp.where(lane < SSM_HEAD_DIM, xp, 0.0),
                                   jnp.where(lane >= SSM_HEAD_DIM, xp, 0.0)], axis=0).astype(BF16)
            ys.append(jnp.dot(jnp.concatenate(ms, axis=1), rhs, preferred_element_type=F32))
        y_ref[:, g * gw:(g + 1) * gw] = (jnp.concatenate(ys, axis=1) + y_off).astype(BF16)
        xdd = (xs_ref[:, g * gw:(g + 1) * gw].astype(F32) * w_state_x[:, g * gw:(g + 1) * gw]).astype(BF16)
        bgT = bg.astype(F32).T.astype(BF16)
        st_ref[g] = (st * e_p_x[end:end + 1, g * gw:(g + 1) * gw]
                     + jnp.dot(bgT, xdd, preferred_element_type=F32))


def _ssd(xs, bm, cm, dt, a_row, ex, reverse, n_ctx_chunks):
    batch, t, _ = xs.shape
    L = SSM_CHUNK
    nc = t // L
    if reverse:
        cmap = lambda i: jnp.where(i < n_ctx_chunks, n_ctx_chunks - 1 - i, nc - 1 + n_ctx_chunks - i)
    else:
        cmap = lambda i: i
    tok = lambda n: pl.BlockSpec((None, L, n), lambda b, i: (b, cmap(i), 0))
    kern = functools.partial(_ssd_kernel, reverse=reverse, lane_base=SSM_HEADS if reverse else 0)
    return pl.pallas_call(
        kern,
        grid=(batch, nc),
        in_specs=[tok(SSM_INNER), tok(SSM_BC_WIDTH), tok(SSM_BC_WIDTH), tok(LANES),
                  _const_spec((1, LANES)), _const_spec((2 * LANES, SSM_INNER))],
        out_specs=tok(SSM_INNER),
        out_shape=jax.ShapeDtypeStruct((batch, t, SSM_INNER), BF16),
        scratch_shapes=[pltpu.VMEM((SSM_GROUPS, SSM_STATE, SSM_INNER // SSM_GROUPS), F32)],
        compiler_params=_cparams(2),
        name="ssd_bwd" if reverse else "ssd_fwd",
    )(xs, bm, cm, dt, a_row, ex)


def _ssm_out_kernel(yf_ref, yb_ref, xs_ref, z_ref, dsk_ref, ng_ref, wo_ref, x_ref, mod_ref, o_ref):
    y = yf_ref[...].astype(F32) + yb_ref[...].astype(F32) + dsk_ref[...] * xs_ref[...].astype(F32)
    y = y * _silu(z_ref[...].astype(F32))
    gw = SSM_INNER // SSM_GROUPS
    parts = []
    for g in range(SSM_GROUPS):
        blk = y[:, g * gw:(g + 1) * gw]
        ms = jnp.mean(blk * blk, axis=-1, keepdims=True)
        parts.append(blk * lax.rsqrt(ms + NORM_EPS))
    yn = (jnp.concatenate(parts, axis=1) * ng_ref[...]).astype(BF16)
    out = jnp.dot(yn, wo_ref[...], preferred_element_type=F32)
    o_ref[...] = x_ref[...] + mod_ref[:, 2 * D_MODEL:3 * D_MODEL] * out


def _ssm_out(yf, yb, xs, z, dsk, ng, w_out, h, mods, n_ctx_tiles):
    batch, t, _ = h.shape
    tm = TOKEN_TILE
    tok = lambda n: pl.BlockSpec((None, tm, n), lambda b, i: (b, i, 0))
    return pl.pallas_call(
        _ssm_out_kernel,
        grid=(batch, t // tm),
        in_specs=[tok(SSM_INNER), tok(SSM_INNER), tok(SSM_INNER), tok(SSM_INNER),
                  _const_spec((1, SSM_INNER)), _const_spec((1, SSM_INNER)),
                  _const_spec((SSM_INNER, D_MODEL)), tok(D_MODEL), _mod_spec(n_ctx_tiles, batch)],
        out_specs=tok(D_MODEL),
        out_shape=jax.ShapeDtypeStruct(h.shape, F32),
        compiler_params=_cparams(2),
        name="ssm_out",
    )(yf, yb, xs, z, dsk, ng, w_out, h, mods)


def _rope_tables(ctx_len, seq):
    tpos = jnp.arange(seq)
    row = (tpos // GRID_W).astype(F32)
    col = (tpos % GRID_W).astype(F32)
    n_freq = HEAD_DIM // 4
    inv = ROPE_THETA ** (-jnp.arange(n_freq, dtype=F32) / n_freq)
    ang = jnp.concatenate([row[:, None] * inv, col[:, None] * inv], axis=-1)
    cos = jnp.cos(ang)
    sin = jnp.sin(ang)
    cos64 = jnp.concatenate([cos, cos], axis=-1)
    sin64 = jnp.concatenate([-sin, sin], axis=-1)
    cos_t = jnp.concatenate([jnp.ones((ctx_len, HEAD_DIM), F32), cos64], axis=0)
    sin_t = jnp.concatenate([jnp.zeros((ctx_len, HEAD_DIM), F32), sin64], axis=0)
    return jnp.tile(cos_t, (1, N_KV_HEADS)), jnp.tile(sin_t, (1, N_KV_HEADS))


def _block_mean_matrix(n, group):
    idx = np.arange(n) // group
    return jnp.asarray((idx[:, None] == idx[None, :]).astype(np.float32) / group, dtype=BF16)


def kernel(x, c, ctx, c_ctx, ada_w, ada_b, norm1_g, norm2_g, ffn_w_in, ffn_w_out, mix_w_in, mix_w_out,
           q_norm_g, k_norm_g, sgu_norm_g, sgu_w, sgu_b, ssm_w_in, ssm_conv_w, ssm_conv_b, ssm_dt_bias,
           ssm_a_log, ssm_d, ssm_norm_g, ssm_w_out):
    batch, seq, _ = x.shape
    ctx_len = ctx.shape[1]
    depth = ada_w.shape[0]
    assert ctx_len % TOKEN_TILE == 0 and seq % TOKEN_TILE == 0 and batch < SUBLANES
    n_ctx_tiles = ctx_len // TOKEN_TILE
    n_ctx_chunks = ctx_len // SSM_CHUNK

    h = jnp.concatenate([ctx, x], axis=1)
    cond = jnp.zeros((SUBLANES, D_MODEL), F32).at[:batch].set(c).at[batch].set(c_ctx)
    mods = _adaln_all(cond, ada_w, ada_b)[:, :, None, :]

    cos_t, sin_t = _rope_tables(ctx_len, seq)
    gq = _block_mean_matrix(ATTN_WIDTH, HEAD_DIM)
    gk = _block_mean_matrix(KV_WIDTH, HEAD_DIM)
    perm = np.arange(ATTN_WIDTH).reshape(N_KV_HEADS, Q_PER_KV, HEAD_DIM).transpose(1, 0, 2).reshape(-1)
    ex_np = np.zeros((2, 2 * LANES, SSM_INNER), np.float32)
    for d in range(2):
        for hd in range(SSM_HEADS):
            for half in range(2):
                ex_np[d, half * LANES + d * SSM_HEADS + hd, hd * SSM_HEAD_DIM:(hd + 1) * SSM_HEAD_DIM] = 1.0
    ex = jnp.asarray(ex_np, dtype=BF16)

    for i in range(depth):
        j = i // 2
        m = mods[i]
        g1 = norm1_g[i][None, :]
        if i % 2 == 0:
            qT, k, vT, gu, vn = _mix_in(
                h, m, g1, mix_w_in[j].astype(BF16), jnp.tile(q_norm_g[j], N_Q_HEADS)[None, :],
                jnp.tile(k_norm_g[j], N_KV_HEADS)[None, :], cos_t, sin_t, gq, gk,
                sgu_norm_g[j][None, :], n_ctx_tiles)
            score_bound = (1.02 * HEAD_DIM ** 0.5 * LOG2E) * jnp.max(jnp.abs(q_norm_g[j])) * jnp.max(jnp.abs(k_norm_g[j]))
            safe = (score_bound <= SAFE_LOG2_BOUND).astype(jnp.int32).reshape(1)
            att = _attention(safe, qT, k, vT, n_ctx_tiles, ctx_len)
            w_out = mix_w_out[j]
            w_out = jnp.concatenate([w_out[:ATTN_WIDTH][perm], w_out[ATTN_WIDTH:]], axis=0).astype(BF16)
            sb = jnp.repeat(sgu_b[j].T, GMLP_WIDTH // GMLP_GROUPS, axis=1)
            h = _mix_out(att, gu, vn, sgu_w[j].astype(BF16), sb, w_out, h, m, n_ctx_tiles)
        else:
            w_in = ssm_w_in[j]
            wz = w_in[:, :SSM_INNER].astype(BF16)
            wx = w_in[:, SSM_INNER:SSM_INNER + SSM_CONV_DIM].astype(BF16)
            n_dt = 2 * SSM_HEADS
            wdt = jnp.zeros((D_MODEL, LANES), F32).at[:, :n_dt].set(w_in[:, SSM_INNER + SSM_CONV_DIM:]).astype(BF16)
            dtb = jnp.zeros((1, LANES), F32).at[0, :n_dt].set(ssm_dt_bias[j].reshape(-1))
            cw = jnp.zeros((SUBLANES, SSM_CONV_DIM), F32).at[:ssm_conv_w.shape[1]].set(ssm_conv_w[j])
            z, xs, bm, cm, dt = _ssm_in(h, m, g1, wz, wx, wdt, cw, ssm_conv_b[j][None, :], dtb, n_ctx_tiles)
            a = -jnp.exp(ssm_a_log[j].astype(F32))
            ys = []
            for d in range(2):
                a_row = jnp.zeros((1, LANES), F32).at[0, d * SSM_HEADS:(d + 1) * SSM_HEADS].set(a[d])
                ys.append(_ssd(xs, bm, cm, dt, a_row, ex[d], d == 1, n_ctx_chunks))
            dsk = jnp.repeat(ssm_d[j], SSM_HEAD_DIM)[None, :]
            h = _ssm_out(ys[0], ys[1], xs, z, dsk, ssm_norm_g[j][None, :], ssm_w_out[j].astype(BF16), h, m,
                         n_ctx_tiles)
        h = _ffn(h, m, norm2_g[i][None, :], ffn_w_in[i].astype(BF16), ffn_w_out[i].astype(BF16), n_ctx_tiles)
    return h[:, ctx_len:, :]
```

```python
import functools
import math

import jax
import jax.numpy as jnp
import numpy as np
from jax import lax
from jax.experimental import pallas as pl
from jax.experimental.pallas import tpu as pltpu

F32 = jnp.float32
BF16 = jnp.bfloat16

D_MODEL = 1024
GRID_W = 64
HEAD_DIM = 64
ATTN_WIDTH = D_MODEL // 2
N_Q_HEADS = ATTN_WIDTH // HEAD_DIM
N_KV_HEADS = 2
Q_PER_KV = N_Q_HEADS // N_KV_HEADS
KV_WIDTH = N_KV_HEADS * HEAD_DIM
ROPE_THETA = 10000.0
GMLP_WIDTH = D_MODEL // 2
GMLP_CHUNK = 128
GMLP_GROUPS = 4
MIX_IN = ATTN_WIDTH + 2 * KV_WIDTH + 2 * GMLP_WIDTH
SSM_INNER = 2 * D_MODEL
SSM_HEAD_DIM = 64
SSM_HEADS = SSM_INNER // SSM_HEAD_DIM
SSM_GROUPS = 4
SSM_STATE = 128
SSM_CHUNK = 128
SSM_BC_WIDTH = SSM_GROUPS * SSM_STATE
SSM_CONV_DIM = SSM_INNER + 2 * SSM_BC_WIDTH
FFN_HIDDEN = -(-8 * D_MODEL // (3 * 256)) * 256
NORM_EPS = 1e-6

LANES = 128
SUBLANES = 8
TOKEN_TILE = 256
VMEM_LIMIT = 56 * 1024 * 1024
LOG2E = 1.4426950408889634
BF16_ROWS = 16
V_ROWS = HEAD_DIM + BF16_ROWS
SAFE_LOG2_BOUND = 60.0


def _cparams(n_axes):
    return pltpu.CompilerParams(dimension_semantics=("arbitrary",) * n_axes,
                                vmem_limit_bytes=VMEM_LIMIT)


def _silu(x):
    return x / (1.0 + jnp.exp(-x))


def _gelu_tanh(x):
    c = math.sqrt(2.0 / math.pi)
    return x * (0.5 * (1.0 + jnp.tanh(c * (x + 0.044715 * (x * x * x)))))


def _rms_mod(x, g, shift, scale):
    ms = jnp.mean(x * x, axis=-1, keepdims=True)
    y = (x * lax.rsqrt(ms + NORM_EPS)) * g
    return y * (1.0 + scale) + shift


def _split2(x):
    hi = x.astype(BF16)
    lo = (x - hi.astype(F32)).astype(BF16)
    return hi, lo


def _group_mean(x2, gmat):
    hi, lo = _split2(x2)
    return (jnp.dot(hi, gmat, preferred_element_type=F32)
            + jnp.dot(lo, gmat, preferred_element_type=F32))


def _rope(x, cos, sin_signed):
    n = x.shape[-1]
    half = HEAD_DIM // 2
    fwd = pltpu.roll(x, n - half, axis=1)
    bwd = pltpu.roll(x, half, axis=1)
    lane = lax.broadcasted_iota(jnp.int32, x.shape, 1)
    partner = jnp.where((lane % HEAD_DIM) < half, fwd, bwd)
    return x * cos + partner * sin_signed


def _adaln_kernel(c_ref, w_ref, b_ref, o_ref):
    s = _silu(c_ref[...]).astype(BF16)
    o_ref[...] = jnp.dot(s, w_ref[...].astype(BF16), preferred_element_type=F32) + b_ref[...]


def _adaln_all(cond, ada_w, ada_b):
    depth = ada_w.shape[0]
    tn = 1536
    return pl.pallas_call(
        _adaln_kernel,
        grid=(depth, 6 * D_MODEL // tn),
        in_specs=[pl.BlockSpec((SUBLANES, D_MODEL), lambda l, j: (0, 0)),
                  pl.BlockSpec((None, D_MODEL, tn), lambda l, j: (l, 0, j)),
                  pl.BlockSpec((None, 1, tn), lambda l, j: (l, 0, j))],
        out_specs=pl.BlockSpec((None, SUBLANES, tn), lambda l, j: (l, 0, j)),
        out_shape=jax.ShapeDtypeStruct((depth, SUBLANES, 6 * D_MODEL), F32),
        compiler_params=_cparams(2),
        name="adaln",
    )(cond, ada_w, ada_b.reshape(depth, 1, 6 * D_MODEL))


def _mod_spec(n_ctx_tiles, batch):
    return pl.BlockSpec((None, 1, 6 * D_MODEL),
                        lambda b, i: (jnp.where(i < n_ctx_tiles, batch, b), 0, 0))


def _const_spec(shape):
    nd = len(shape)
    return pl.BlockSpec(shape, lambda b, i: (0,) * nd)


def _mix_in_kernel(x_ref, mod_ref, g1_ref, w_ref, qg_ref, kg_ref, cos_ref, sin_ref, gq_ref, gk_ref,
                   sg_ref, qT_ref, k_ref, vT_ref, gu_ref, vn_ref):
    mod = mod_ref[...]
    h = _rms_mod(x_ref[...], g1_ref[...], mod[:, 0:D_MODEL], mod[:, D_MODEL:2 * D_MODEL]).astype(BF16)
    p = jnp.dot(h, w_ref[...], preferred_element_type=F32)
    q = p[:, 0:ATTN_WIDTH]
    k = p[:, ATTN_WIDTH:ATTN_WIDTH + KV_WIDTH]
    v = p[:, ATTN_WIDTH + KV_WIDTH:ATTN_WIDTH + 2 * KV_WIDTH]
    gm = p[:, ATTN_WIDTH + 2 * KV_WIDTH:]
    cos = cos_ref[...]
    sin = sin_ref[...]
    q = q * lax.rsqrt(_group_mean(q * q, gq_ref[...]) + NORM_EPS) * qg_ref[...]
    k = k * lax.rsqrt(_group_mean(k * k, gk_ref[...]) + NORM_EPS) * kg_ref[...]
    reps = ATTN_WIDTH // KV_WIDTH
    q = _rope(q, jnp.concatenate([cos] * reps, axis=1), jnp.concatenate([sin] * reps, axis=1))
    k = _rope(k, cos, sin)
    q = q * (HEAD_DIM ** -0.5 * LOG2E)
    qT_ref[...] = q.T.astype(BF16)
    k_ref[...] = k.astype(BF16)
    vT = v.T
    ones = jnp.ones((BF16_ROWS, vT.shape[1]), F32)
    vT_ref[...] = jnp.concatenate(
        [blk for kvh in range(N_KV_HEADS) for blk in (vT[kvh * HEAD_DIM:(kvh + 1) * HEAD_DIM, :], ones)],
        axis=0).astype(BF16)
    ge = _gelu_tanh(gm)
    gu_ref[...] = ge[:, 0:GMLP_WIDTH].astype(BF16)
    gv = ge[:, GMLP_WIDTH:]
    gd = GMLP_WIDTH // GMLP_GROUPS
    parts = []
    for g in range(GMLP_GROUPS):
        blk = gv[:, g * gd:(g + 1) * gd]
        ms = jnp.mean(blk * blk, axis=-1, keepdims=True)
        parts.append(blk * lax.rsqrt(ms + NORM_EPS))
    vn_ref[...] = (jnp.concatenate(parts, axis=1) * sg_ref[...]).astype(BF16)


def _mix_in(h, mods, g1, w_in, qg, kg, cos, sin, gq, gk, sgg, n_ctx_tiles):
    batch, t, _ = h.shape
    tm = TOKEN_TILE
    tok = lambda n: pl.BlockSpec((None, tm, n), lambda b, i: (b, i, 0))
    tr = lambda n: pl.BlockSpec((None, n, tm), lambda b, i: (b, 0, i))
    return pl.pallas_call(
        _mix_in_kernel,
        grid=(batch, t // tm),
        in_specs=[tok(D_MODEL), _mod_spec(n_ctx_tiles, batch), _const_spec((1, D_MODEL)),
                  _const_spec((D_MODEL, MIX_IN)), _const_spec((1, ATTN_WIDTH)), _const_spec((1, KV_WIDTH)),
                  pl.BlockSpec((tm, KV_WIDTH), lambda b, i: (i, 0)),
                  pl.BlockSpec((tm, KV_WIDTH), lambda b, i: (i, 0)),
                  _const_spec((ATTN_WIDTH, ATTN_WIDTH)), _const_spec((KV_WIDTH, KV_WIDTH)),
                  _const_spec((1, GMLP_WIDTH))],
        out_specs=[tr(ATTN_WIDTH), tok(KV_WIDTH), tr(N_KV_HEADS * V_ROWS), tok(GMLP_WIDTH), tok(GMLP_WIDTH)],
        out_shape=[jax.ShapeDtypeStruct((batch, ATTN_WIDTH, t), BF16),
                   jax.ShapeDtypeStruct((batch, t, KV_WIDTH), BF16),
                   jax.ShapeDtypeStruct((batch, N_KV_HEADS * V_ROWS, t), BF16),
                   jax.ShapeDtypeStruct((batch, t, GMLP_WIDTH), BF16),
                   jax.ShapeDtypeStruct((batch, t, GMLP_WIDTH), BF16)],
        compiler_params=_cparams(2),
        name="mix_in",
    )(h, mods, g1, w_in, qg, kg, cos, sin, gq, gk, sgg)


def _attn_kernel(safe_ref, qT_ref, k_ref, vT_ref, o_ref, qpad_ref, acc_ref, m_ref, *, tq, tk, tk_main,
                 n_ctx_tiles, ctx_len, t):
    i = pl.program_id(1)
    for kvh in range(N_KV_HEADS):
        qcat = jnp.concatenate(
            [qT_ref[(kvh * Q_PER_KV + g) * HEAD_DIM:(kvh * Q_PER_KV + g + 1) * HEAD_DIM, :]
             for g in range(Q_PER_KV)], axis=1)
        z = jnp.zeros_like(qcat)
        qpad_ref[kvh] = jnp.concatenate([qcat, z] if kvh == 0 else [z, qcat], axis=0)
    acc_ref[...] = jnp.zeros(acc_ref.shape, F32)
    is_latent = i >= n_ctx_tiles
    safe = safe_ref[0] != 0

    def plain_step(off, width):
        kblk = k_ref[pl.ds(off, width), :]
        for kvh in range(N_KV_HEADS):
            s = jnp.dot(kblk, qpad_ref[kvh], preferred_element_type=F32)
            p = jnp.exp2(s).astype(BF16)
            vblk = vT_ref[kvh * V_ROWS:(kvh + 1) * V_ROWS, pl.ds(off, width)]
            acc_ref[kvh] += jnp.dot(vblk, p, preferred_element_type=F32)

    @pl.when(safe)
    def _():
        plain_step(0, ctx_len)

        @pl.when(is_latent)
        def _():
            def body(j, carry):
                plain_step(pl.multiple_of(ctx_len + j * tk_main, LANES), tk_main)
                return carry
            lax.fori_loop(0, (t - ctx_len) // tk_main, body, 0, unroll=2)

    @pl.when(jnp.logical_not(safe))
    def _():
        m_ref[...] = jnp.full(m_ref.shape, -jnp.inf, F32)

        def body(j, carry):
            off = pl.multiple_of(j * tk, tk)
            kblk = k_ref[pl.ds(off, tk), :]
            for kvh in range(N_KV_HEADS):
                s = jnp.dot(kblk, qpad_ref[kvh], preferred_element_type=F32)
                m_old = m_ref[kvh]
                m_new = jnp.maximum(m_old, jnp.max(s, axis=0, keepdims=True))
                alpha = jnp.exp2(m_old - m_new)
                p = jnp.exp2(s - m_new).astype(BF16)
                vblk = vT_ref[kvh * V_ROWS:(kvh + 1) * V_ROWS, pl.ds(off, tk)]
                acc_ref[kvh] = acc_ref[kvh] * alpha + jnp.dot(vblk, p, preferred_element_type=F32)
                m_ref[kvh] = m_new
            return carry

        lax.fori_loop(0, jnp.where(is_latent, t // tk, ctx_len // tk), body, 0)

    for g in range(Q_PER_KV):
        og = jnp.concatenate(
            [acc_ref[kvh, 0:HEAD_DIM, g * tq:(g + 1) * tq]
             / acc_ref[kvh, HEAD_DIM:HEAD_DIM + 1, g * tq:(g + 1) * tq]
             for kvh in range(N_KV_HEADS)], axis=0)
        o_ref[:, g * KV_WIDTH:(g + 1) * KV_WIDTH] = og.T.astype(BF16)


def _attention(safe, qT, k, vT, n_ctx_tiles, ctx_len):
    batch, _, t = qT.shape
    tq = TOKEN_TILE
    tk = 256
    tk_main = 512
    assert ctx_len % tk == 0 and t % tk == 0 and (t - ctx_len) % tk_main == 0
    kern = functools.partial(_attn_kernel, tq=tq, tk=tk, tk_main=tk_main, n_ctx_tiles=n_ctx_tiles,
                             ctx_len=ctx_len, t=t)
    return pl.pallas_call(
        kern,
        grid=(batch, t // tq),
        in_specs=[pl.BlockSpec(memory_space=pltpu.SMEM),
                  pl.BlockSpec((None, ATTN_WIDTH, tq), lambda b, i: (b, 0, i)),
                  pl.BlockSpec((None, t, KV_WIDTH), lambda b, i: (b, 0, 0)),
                  pl.BlockSpec((None, N_KV_HEADS * V_ROWS, t), lambda b, i: (b, 0, 0))],
        out_specs=pl.BlockSpec((None, tq, ATTN_WIDTH), lambda b, i: (b, i, 0)),
        out_shape=jax.ShapeDtypeStruct((batch, t, ATTN_WIDTH), BF16),
        scratch_shapes=[pltpu.VMEM((N_KV_HEADS, KV_WIDTH, Q_PER_KV * tq), BF16),
                        pltpu.VMEM((N_KV_HEADS, V_ROWS, Q_PER_KV * tq), F32),
                        pltpu.VMEM((N_KV_HEADS, 1, Q_PER_KV * tq), F32)],
        compiler_params=_cparams(2),
        name="attention",
    )(safe, qT, k, vT)


FFN_SPLIT = 2


def _residual_ffn(x, y, mod, g2_ref, wi_ref, wf_ref):
    x = x + mod[:, 2 * D_MODEL:3 * D_MODEL] * y
    h = _rms_mod(x, g2_ref[...], mod[:, 3 * D_MODEL:4 * D_MODEL], mod[:, 4 * D_MODEL:5 * D_MODEL]).astype(BF16)
    hc = FFN_HIDDEN // FFN_SPLIT
    acc = jnp.zeros(x.shape, F32)
    for c in range(FFN_SPLIT):
        g = jnp.dot(h, wi_ref[:, c * hc:(c + 1) * hc], preferred_element_type=F32)
        u = jnp.dot(h, wi_ref[:, FFN_HIDDEN + c * hc:FFN_HIDDEN + (c + 1) * hc], preferred_element_type=F32)
        a = (_silu(g) * u).astype(BF16)
        acc = acc + jnp.dot(a, wf_ref[c * hc:(c + 1) * hc, :], preferred_element_type=F32)
    return x + mod[:, 5 * D_MODEL:6 * D_MODEL] * acc


def _mix_tail_kernel(att_ref, gu_ref, vn_ref, x_ref, mod_ref, sw_ref, sb_ref, wo_ref, g2_ref, wi_ref, wf_ref,
                     o_ref):
    tm = x_ref.shape[0]
    gd = GMLP_WIDTH // GMLP_GROUPS
    chunks = []
    for c in range(tm // GMLP_CHUNK):
        rows = slice(c * GMLP_CHUNK, (c + 1) * GMLP_CHUNK)
        s = jnp.concatenate(
            [jnp.dot(sw_ref[g], vn_ref[rows, g * gd:(g + 1) * gd], preferred_element_type=F32)
             for g in range(GMLP_GROUPS)], axis=1) + sb_ref[...]
        chunks.append(gu_ref[rows, :].astype(F32) * s)
    sg = jnp.concatenate(chunks, axis=0).astype(BF16)
    y = (jnp.dot(att_ref[...], wo_ref[0:ATTN_WIDTH, :], preferred_element_type=F32)
         + jnp.dot(sg, wo_ref[ATTN_WIDTH:, :], preferred_element_type=F32))
    o_ref[...] = _residual_ffn(x_ref[...], y, mod_ref[...], g2_ref, wi_ref, wf_ref)


def _ssm_tail_kernel(yf_ref, yb_ref, xs_ref, z_ref, x_ref, mod_ref, dsk_ref, ng_ref, wo_ref, g2_ref, wi_ref,
                     wf_ref, o_ref):
    y = yf_ref[...].astype(F32) + yb_ref[...].astype(F32) + dsk_ref[...] * xs_ref[...].astype(F32)
    y = y * _silu(z_ref[...].astype(F32))
    gw = SSM_INNER // SSM_GROUPS
    parts = []
    for g in range(SSM_GROUPS):
        blk = y[:, g * gw:(g + 1) * gw]
        ms = jnp.mean(blk * blk, axis=-1, keepdims=True)
        parts.append(blk * lax.rsqrt(ms + NORM_EPS))
    yn = (jnp.concatenate(parts, axis=1) * ng_ref[...]).astype(BF16)
    out = jnp.dot(yn, wo_ref[...], preferred_element_type=F32)
    o_ref[...] = _residual_ffn(x_ref[...], out, mod_ref[...], g2_ref, wi_ref, wf_ref)


def _layer_tail(kern, name, tok_inputs, h, mods, consts, n_ctx_tiles, latent_only):
    batch, t, _ = h.shape
    tm = TOKEN_TILE
    skip = n_ctx_tiles if latent_only else 0
    tok = lambda n: pl.BlockSpec((None, tm, n), lambda b, i: (b, i + skip, 0))
    mod_spec = pl.BlockSpec((None, 1, 6 * D_MODEL),
                            lambda b, i: (jnp.where(i + skip < n_ctx_tiles, batch, b), 0, 0))
    resident = lambda a: pl.BlockSpec(a.shape, lambda b, i, nd=a.ndim: (0,) * nd, pipeline_mode=pl.Buffered(1))
    return pl.pallas_call(
        kern,
        grid=(batch, t // tm - skip),
        in_specs=([tok(a.shape[-1]) for a in tok_inputs] + [tok(D_MODEL), mod_spec]
                  + [resident(a) for a in consts]),
        out_specs=pl.BlockSpec((None, tm, D_MODEL), lambda b, i: (b, i, 0)),
        out_shape=jax.ShapeDtypeStruct((batch, t - skip * tm, D_MODEL), F32),
        compiler_params=_cparams(2),
        name=name,
    )(*tok_inputs, h, mods, *consts)


def _ssm_in_kernel(xp_ref, x_ref, xn_ref, mod_ref, g1_ref, wz_ref, wx_ref, wdt_ref, cw_ref, cb_ref, dtb_ref,
                   z_ref, xs_ref, bm_ref, cm_ref, dt_ref, *, n_tiles, n_ctx_tiles, col_block):
    i = pl.program_id(1)
    tm = x_ref.shape[0]
    halo = SUBLANES
    prev_ok = jnp.logical_and(i != 0, i != n_ctx_tiles)
    next_ok = jnp.logical_and(i != n_ctx_tiles - 1, i != n_tiles - 1)
    xcat = jnp.concatenate([x_ref[...], xn_ref[...], xp_ref[...]], axis=0)
    mod = mod_ref[...]
    h = _rms_mod(xcat, g1_ref[...], mod[:, 0:D_MODEL], mod[:, D_MODEL:2 * D_MODEL]).astype(BF16)
    hm = h[0:tm, :]
    z_ref[...] = jnp.dot(hm, wz_ref[...], preferred_element_type=F32).astype(BF16)
    dtr = jnp.dot(hm, wdt_ref[...], preferred_element_type=F32) + dtb_ref[...]
    dt_ref[...] = jnp.maximum(dtr, 0.0) + jnp.log1p(jnp.exp(-jnp.abs(dtr)))
    prev_f = prev_ok.astype(F32)
    next_f = next_ok.astype(F32)
    for blk in range(SSM_CONV_DIM // col_block):
        cols = slice(blk * col_block, (blk + 1) * col_block)
        p = jnp.dot(h, wx_ref[:, cols], preferred_element_type=F32)
        p = jnp.concatenate([p[0:tm, :], p[tm:tm + halo, :] * next_f, p[tm + halo:, :] * prev_f], axis=0)
        pm1 = pltpu.roll(p, 1, axis=0)[0:tm, :]
        pp1 = pltpu.roll(p, tm + 2 * halo - 1, axis=0)[0:tm, :]
        y = (cw_ref[0:1, cols] * pm1 + cw_ref[1:2, cols] * p[0:tm, :]
             + cw_ref[2:3, cols] * pp1 + cb_ref[:, cols])
        y = _silu(y).astype(BF16)
        c0 = blk * col_block
        if c0 < SSM_INNER:
            xs_ref[:, c0:c0 + col_block] = y
        elif c0 < SSM_INNER + SSM_BC_WIDTH:
            bm_ref[:, c0 - SSM_INNER:c0 - SSM_INNER + col_block] = y
        else:
            o0 = c0 - SSM_INNER - SSM_BC_WIDTH
            cm_ref[:, o0:o0 + col_block] = y


def _ssm_in(h, mods, g1, wz, wx, wdt, cw, cb, dtb, n_ctx_tiles):
    batch, t, _ = h.shape
    tm = TOKEN_TILE
    n_tiles = t // tm
    per = tm // SUBLANES
    tok = lambda n: pl.BlockSpec((None, tm, n), lambda b, i: (b, i, 0))
    col_block = 512
    kern = functools.partial(_ssm_in_kernel, n_tiles=n_tiles, n_ctx_tiles=n_ctx_tiles, col_block=col_block)
    return pl.pallas_call(
        kern,
        grid=(batch, n_tiles),
        in_specs=[pl.BlockSpec((None, SUBLANES, D_MODEL), lambda b, i: (b, jnp.maximum(i * per - 1, 0), 0)),
                  tok(D_MODEL),
                  pl.BlockSpec((None, SUBLANES, D_MODEL),
                               lambda b, i: (b, jnp.minimum((i + 1) * per, t // SUBLANES - 1), 0)),
                  _mod_spec(n_ctx_tiles, batch), _const_spec((1, D_MODEL)),
                  _const_spec((D_MODEL, SSM_INNER)), _const_spec((D_MODEL, SSM_CONV_DIM)),
                  _const_spec((D_MODEL, LANES)), _const_spec((SUBLANES, SSM_CONV_DIM)),
                  _const_spec((1, SSM_CONV_DIM)), _const_spec((1, LANES))],
        out_specs=[tok(SSM_INNER), tok(SSM_INNER), tok(SSM_BC_WIDTH), tok(SSM_BC_WIDTH), tok(LANES)],
        out_shape=[jax.ShapeDtypeStruct((batch, t, SSM_INNER), BF16),
                   jax.ShapeDtypeStruct((batch, t, SSM_INNER), BF16),
                   jax.ShapeDtypeStruct((batch, t, SSM_BC_WIDTH), BF16),
                   jax.ShapeDtypeStruct((batch, t, SSM_BC_WIDTH), BF16),
                   jax.ShapeDtypeStruct((batch, t, LANES), F32)],
        compiler_params=_cparams(2),
        name="ssm_in",
    )(h, h, h, mods, g1, wz, wx, wdt, cw, cb, dtb)


def _ssd_kernel(xs_ref, bm_ref, cm_ref, dt_ref, a_ref, ex_ref, y_ref, st_ref, *, reverse, lane_base):
    L = SSM_CHUNK

    @pl.when(pl.program_id(1) == 0)
    def _():
        st_ref[...] = jnp.zeros(st_ref.shape, F32)

    dt = dt_ref[...]
    da = dt * a_ref[...]
    r = lax.broadcasted_iota(jnp.int32, (L, L), 0)
    c = lax.broadcasted_iota(jnp.int32, (L, L), 1)
    before = (c >= r) if reverse else (c <= r)
    tri = jnp.where(before, 1.0, 0.0).astype(BF16)
    hi = da.astype(BF16)
    r1 = da - hi.astype(F32)
    mid = r1.astype(BF16)
    lo = (r1 - mid.astype(F32)).astype(BF16)
    p3 = jnp.dot(tri, jnp.concatenate([hi, mid, lo], axis=1), preferred_element_type=F32)
    P = p3[:, 0:LANES] + p3[:, LANES:2 * LANES] + p3[:, 2 * LANES:3 * LANES]
    tot = jnp.sum(da, axis=0, keepdims=True)
    QT = (P - jnp.log(dt)).T

    def expand(v):
        return jnp.dot(jnp.concatenate(_split2(v), axis=1), ex_ref[...], preferred_element_type=F32)

    w_state_all = expand(dt * jnp.exp(tot - P))
    e_p_all = expand(jnp.exp(P))
    end = 0 if reverse else L - 1
    lane = lax.broadcasted_iota(jnp.int32, (1, LANES), 1)
    first_head = jnp.where(lane < SSM_HEAD_DIM, 1.0, 0.0).astype(BF16)
    second_head = jnp.where(lane >= SSM_HEAD_DIM, 1.0, 0.0).astype(BF16)
    heads_per_group = SSM_HEADS // SSM_GROUPS
    gw = heads_per_group * SSM_HEAD_DIM
    for g in range(SSM_GROUPS):
        e_p_x = e_p_all[:, g * gw:(g + 1) * gw]
        bg = bm_ref[:, g * SSM_STATE:(g + 1) * SSM_STATE]
        cg = cm_ref[:, g * SSM_STATE:(g + 1) * SSM_STATE]
        cb = lax.dot_general(cg, bg, (((1,), (1,)), ((), ())), preferred_element_type=F32)
        st = st_ref[g]
        y_off = jnp.dot(cg, st.astype(BF16), preferred_element_type=F32) * e_p_x
        ys = []
        for pair in range(heads_per_group // 2):
            ms = []
            for k in range(2):
                hl = lane_base + g * heads_per_group + 2 * pair + k
                seg = P[:, hl:hl + 1] - QT[hl:hl + 1, :]
                ms.append((cb * jnp.exp(jnp.where(before, seg, -jnp.inf))).astype(BF16))
            xp = xs_ref[:, g * gw + pair * LANES:g * gw + (pair + 1) * LANES]
            rhs = jnp.concatenate([xp * first_head, xp * second_head], axis=0)
            ys.append(jnp.dot(jnp.concatenate(ms, axis=1), rhs, preferred_element_type=F32))
        y_ref[:, g * gw:(g + 1) * gw] = (jnp.concatenate(ys, axis=1) + y_off).astype(BF16)
        xdd = (xs_ref[:, g * gw:(g + 1) * gw].astype(F32) * w_state_all[:, g * gw:(g + 1) * gw]).astype(BF16)
        bgT = bg.astype(F32).T.astype(BF16)
        st_ref[g] = st * e_p_x[end:end + 1, :] + jnp.dot(bgT, xdd, preferred_element_type=F32)


def _ssd(xs, bm, cm, dt, a_row, ex, reverse, n_ctx_chunks):
    batch, t, _ = xs.shape
    L = SSM_CHUNK
    nc = t // L
    if reverse:
        cmap = lambda i: jnp.where(i < n_ctx_chunks, n_ctx_chunks - 1 - i, nc - 1 + n_ctx_chunks - i)
    else:
        cmap = lambda i: i
    tok = lambda n: pl.BlockSpec((None, L, n), lambda b, i: (b, cmap(i), 0))
    kern = functools.partial(_ssd_kernel, reverse=reverse, lane_base=SSM_HEADS if reverse else 0)
    return pl.pallas_call(
        kern,
        grid=(batch, nc),
        in_specs=[tok(SSM_INNER), tok(SSM_BC_WIDTH), tok(SSM_BC_WIDTH), tok(LANES),
                  _const_spec((1, LANES)), _const_spec((2 * LANES, SSM_INNER))],
        out_specs=tok(SSM_INNER),
        out_shape=jax.ShapeDtypeStruct((batch, t, SSM_INNER), BF16),
        scratch_shapes=[pltpu.VMEM((SSM_GROUPS, SSM_STATE, SSM_INNER // SSM_GROUPS), F32)],
        compiler_params=_cparams(2),
        name="ssd_bwd" if reverse else "ssd_fwd",
    )(xs, bm, cm, dt, a_row, ex)


def _rope_tables(ctx_len, seq):
    tpos = jnp.arange(seq)
    row = (tpos // GRID_W).astype(F32)
    col = (tpos % GRID_W).astype(F32)
    n_freq = HEAD_DIM // 4
    inv = ROPE_THETA ** (-jnp.arange(n_freq, dtype=F32) / n_freq)
    ang = jnp.concatenate([row[:, None] * inv, col[:, None] * inv], axis=-1)
    cos = jnp.cos(ang)
    sin = jnp.sin(ang)
    cos64 = jnp.concatenate([cos, cos], axis=-1)
    sin64 = jnp.concatenate([-sin, sin], axis=-1)
    cos_t = jnp.concatenate([jnp.ones((ctx_len, HEAD_DIM), F32), cos64], axis=0)
    sin_t = jnp.concatenate([jnp.zeros((ctx_len, HEAD_DIM), F32), sin64], axis=0)
    return jnp.tile(cos_t, (1, N_KV_HEADS)), jnp.tile(sin_t, (1, N_KV_HEADS))


def _block_mean_matrix(n, group):
    idx = np.arange(n) // group
    return jnp.asarray((idx[:, None] == idx[None, :]).astype(np.float32) / group, dtype=BF16)


def kernel(x, c, ctx, c_ctx, ada_w, ada_b, norm1_g, norm2_g, ffn_w_in, ffn_w_out, mix_w_in, mix_w_out,
           q_norm_g, k_norm_g, sgu_norm_g, sgu_w, sgu_b, ssm_w_in, ssm_conv_w, ssm_conv_b, ssm_dt_bias,
           ssm_a_log, ssm_d, ssm_norm_g, ssm_w_out):
    batch, seq, _ = x.shape
    ctx_len = ctx.shape[1]
    depth = ada_w.shape[0]
    assert ctx_len % TOKEN_TILE == 0 and seq % TOKEN_TILE == 0 and batch < SUBLANES
    n_ctx_tiles = ctx_len // TOKEN_TILE
    n_ctx_chunks = ctx_len // SSM_CHUNK

    h = jnp.concatenate([ctx, x], axis=1)
    cond = jnp.zeros((SUBLANES, D_MODEL), F32).at[:batch].set(c).at[batch].set(c_ctx)
    mods = _adaln_all(cond, ada_w, ada_b)[:, :, None, :]

    cos_t, sin_t = _rope_tables(ctx_len, seq)
    gq = _block_mean_matrix(ATTN_WIDTH, HEAD_DIM)
    gk = _block_mean_matrix(KV_WIDTH, HEAD_DIM)
    perm = np.arange(ATTN_WIDTH).reshape(N_KV_HEADS, Q_PER_KV, HEAD_DIM).transpose(1, 0, 2).reshape(-1)
    ex_np = np.zeros((2, 2 * LANES, SSM_INNER), np.float32)
    for d in range(2):
        for hd in range(SSM_HEADS):
            for half in range(2):
                ex_np[d, half * LANES + d * SSM_HEADS + hd, hd * SSM_HEAD_DIM:(hd + 1) * SSM_HEAD_DIM] = 1.0
    ex = jnp.asarray(ex_np, dtype=BF16)

    for i in range(depth):
        j = i // 2
        m = mods[i]
        g1 = norm1_g[i][None, :]
        last = i == depth - 1
        ffn_consts = [norm2_g[i][None, :], ffn_w_in[i].astype(BF16), ffn_w_out[i].astype(BF16)]
        if i % 2 == 0:
            qT, k, vT, gu, vn = _mix_in(
                h, m, g1, mix_w_in[j].astype(BF16), jnp.tile(q_norm_g[j], N_Q_HEADS)[None, :],
                jnp.tile(k_norm_g[j], N_KV_HEADS)[None, :], cos_t, sin_t, gq, gk,
                sgu_norm_g[j][None, :], n_ctx_tiles)
            score_bound = (1.02 * HEAD_DIM ** 0.5 * LOG2E) * jnp.max(jnp.abs(q_norm_g[j])) * jnp.max(jnp.abs(k_norm_g[j]))
            safe = (score_bound <= SAFE_LOG2_BOUND).astype(jnp.int32).reshape(1)
            att = _attention(safe, qT, k, vT, n_ctx_tiles, ctx_len)
            w_out = mix_w_out[j]
            w_out = jnp.concatenate([w_out[:ATTN_WIDTH][perm], w_out[ATTN_WIDTH:]], axis=0).astype(BF16)
            sb = jnp.repeat(sgu_b[j].T, GMLP_WIDTH // GMLP_GROUPS, axis=1)
            h = _layer_tail(_mix_tail_kernel, "mix_tail", [att, gu, vn], h, m,
                            [sgu_w[j].astype(BF16), sb, w_out] + ffn_consts, n_ctx_tiles, last)
        else:
            w_in = ssm_w_in[j]
            wz = w_in[:, :SSM_INNER].astype(BF16)
            wx = w_in[:, SSM_INNER:SSM_INNER + SSM_CONV_DIM].astype(BF16)
            n_dt = 2 * SSM_HEADS
            wdt = jnp.zeros((D_MODEL, LANES), F32).at[:, :n_dt].set(w_in[:, SSM_INNER + SSM_CONV_DIM:]).astype(BF16)
            dtb = jnp.zeros((1, LANES), F32).at[0, :n_dt].set(ssm_dt_bias[j].reshape(-1))
            cw = jnp.zeros((SUBLANES, SSM_CONV_DIM), F32).at[:ssm_conv_w.shape[1]].set(ssm_conv_w[j])
            z, xs, bm, cm, dt = _ssm_in(h, m, g1, wz, wx, wdt, cw, ssm_conv_b[j][None, :], dtb, n_ctx_tiles)
            a = -jnp.exp(ssm_a_log[j].astype(F32))
            ys = []
            for d in range(2):
                a_row = jnp.zeros((1, LANES), F32).at[0, d * SSM_HEADS:(d + 1) * SSM_HEADS].set(a[d])
                ys.append(_ssd(xs, bm, cm, dt, a_row, ex[d], d == 1, n_ctx_chunks))
            dsk = jnp.repeat(ssm_d[j], SSM_HEAD_DIM)[None, :]
            h = _layer_tail(_ssm_tail_kernel, "ssm_tail", [ys[0], ys[1], xs, z], h, m,
                            [dsk, ssm_norm_g[j][None, :], ssm_w_out[j].astype(BF16)] + ffn_consts,
                            n_ctx_tiles, last)
    return h
```

```python
import functools
import math

import jax
import jax.numpy as jnp
import numpy as np
from jax import lax
from jax.experimental import pallas as pl
from jax.experimental.pallas import tpu as pltpu

F32 = jnp.float32
BF16 = jnp.bfloat16

D_MODEL = 1024
GRID_W = 64
HEAD_DIM = 64
ATTN_WIDTH = D_MODEL // 2
N_Q_HEADS = ATTN_WIDTH // HEAD_DIM
N_KV_HEADS = 2
Q_PER_KV = N_Q_HEADS // N_KV_HEADS
KV_WIDTH = N_KV_HEADS * HEAD_DIM
ROPE_THETA = 10000.0
GMLP_WIDTH = D_MODEL // 2
GMLP_CHUNK = 128
GMLP_GROUPS = 4
MIX_IN = ATTN_WIDTH + 2 * KV_WIDTH + 2 * GMLP_WIDTH
SSM_INNER = 2 * D_MODEL
SSM_HEAD_DIM = 64
SSM_HEADS = SSM_INNER // SSM_HEAD_DIM
SSM_GROUPS = 4
SSM_STATE = 128
SSM_CHUNK = 128
SSM_BC_WIDTH = SSM_GROUPS * SSM_STATE
SSM_CONV_DIM = SSM_INNER + 2 * SSM_BC_WIDTH
FFN_HIDDEN = -(-8 * D_MODEL // (3 * 256)) * 256
NORM_EPS = 1e-6

LANES = 128
SUBLANES = 8
TOKEN_TILE = 256
VMEM_LIMIT = 56 * 1024 * 1024
LOG2E = 1.4426950408889634
BF16_ROWS = 16
V_ROWS = HEAD_DIM + BF16_ROWS
SAFE_LOG2_BOUND = 60.0


def _cparams(n_axes):
    return pltpu.CompilerParams(dimension_semantics=("arbitrary",) * n_axes,
                                vmem_limit_bytes=VMEM_LIMIT)


def _silu(x):
    return x / (1.0 + jnp.exp(-x))


def _gelu_tanh(x):
    c = math.sqrt(2.0 / math.pi)
    return x * (0.5 * (1.0 + jnp.tanh(c * (x + 0.044715 * (x * x * x)))))


def _rms_mod(x, g, shift, scale):
    ms = jnp.mean(x * x, axis=-1, keepdims=True)
    return (x * lax.rsqrt(ms + NORM_EPS)) * (g * (1.0 + scale)) + shift


def _split2(x):
    hi = x.astype(BF16)
    lo = (x - hi.astype(F32)).astype(BF16)
    return hi, lo


def _group_mean(x2, gmat):
    hi, lo = _split2(x2)
    return (jnp.dot(hi, gmat, preferred_element_type=F32)
            + jnp.dot(lo, gmat, preferred_element_type=F32))


def _rope(x, cos, sin_signed):
    n = x.shape[-1]
    half = HEAD_DIM // 2
    fwd = pltpu.roll(x, n - half, axis=1)
    bwd = pltpu.roll(x, half, axis=1)
    lane = lax.broadcasted_iota(jnp.int32, x.shape, 1)
    partner = jnp.where((lane % HEAD_DIM) < half, fwd, bwd)
    return x * cos + partner * sin_signed


def _adaln_kernel(c_ref, w_ref, b_ref, o_ref):
    s = _silu(c_ref[...]).astype(BF16)
    o_ref[...] = jnp.dot(s, w_ref[...].astype(BF16), preferred_element_type=F32) + b_ref[...]


def _adaln_all(cond, ada_w, ada_b):
    depth = ada_w.shape[0]
    tn = 1536
    return pl.pallas_call(
        _adaln_kernel,
        grid=(depth, 6 * D_MODEL // tn),
        in_specs=[pl.BlockSpec((SUBLANES, D_MODEL), lambda l, j: (0, 0)),
                  pl.BlockSpec((None, D_MODEL, tn), lambda l, j: (l, 0, j)),
                  pl.BlockSpec((None, 1, tn), lambda l, j: (l, 0, j))],
        out_specs=pl.BlockSpec((None, SUBLANES, tn), lambda l, j: (l, 0, j)),
        out_shape=jax.ShapeDtypeStruct((depth, SUBLANES, 6 * D_MODEL), F32),
        compiler_params=_cparams(2),
        name="adaln",
    )(cond, ada_w, ada_b.reshape(depth, 1, 6 * D_MODEL))


def _mod_spec(n_ctx_tiles, batch):
    return pl.BlockSpec((None, 1, 6 * D_MODEL),
                        lambda b, i: (jnp.where(i < n_ctx_tiles, batch, b), 0, 0))


def _const_spec(shape):
    nd = len(shape)
    return pl.BlockSpec(shape, lambda b, i: (0,) * nd)


def _layer_spec(stacked, layer, **kwargs):
    idx = (layer,) + (0,) * (stacked.ndim - 1)
    return pl.BlockSpec((None,) + stacked.shape[1:], lambda b, i: idx, **kwargs)


def _mix_in_kernel(x_ref, mod_ref, g1_ref, w_ref, qg_ref, kg_ref, cos_ref, sin_ref, gq_ref, gk_ref,
                   sg_ref, qT_ref, k_ref, vT_ref, gu_ref, vn_ref):
    mod = mod_ref[...]
    h = _rms_mod(x_ref[...], g1_ref[...], mod[:, 0:D_MODEL], mod[:, D_MODEL:2 * D_MODEL]).astype(BF16)
    p = jnp.dot(h, w_ref[...], preferred_element_type=F32)
    q = p[:, 0:ATTN_WIDTH]
    k = p[:, ATTN_WIDTH:ATTN_WIDTH + KV_WIDTH]
    v = p[:, ATTN_WIDTH + KV_WIDTH:ATTN_WIDTH + 2 * KV_WIDTH]
    gm = p[:, ATTN_WIDTH + 2 * KV_WIDTH:]
    cos = cos_ref[...]
    sin = sin_ref[...]
    q = q * lax.rsqrt(_group_mean(q * q, gq_ref[...]) + NORM_EPS) * qg_ref[...]
    k = k * lax.rsqrt(_group_mean(k * k, gk_ref[...]) + NORM_EPS) * kg_ref[...]
    reps = ATTN_WIDTH // KV_WIDTH
    q = _rope(q, jnp.concatenate([cos] * reps, axis=1), jnp.concatenate([sin] * reps, axis=1))
    k = _rope(k, cos, sin)
    q = q * (HEAD_DIM ** -0.5 * LOG2E)
    qT_ref[...] = q.T.astype(BF16)
    k_ref[...] = k.astype(BF16)
    vT = v.T
    ones = jnp.ones((BF16_ROWS, vT.shape[1]), F32)
    vT_ref[...] = jnp.concatenate(
        [blk for kvh in range(N_KV_HEADS) for blk in (vT[kvh * HEAD_DIM:(kvh + 1) * HEAD_DIM, :], ones)],
        axis=0).astype(BF16)
    ge = _gelu_tanh(gm)
    gu_ref[...] = ge[:, 0:GMLP_WIDTH].astype(BF16)
    gv = ge[:, GMLP_WIDTH:]
    gd = GMLP_WIDTH // GMLP_GROUPS
    parts = []
    for g in range(GMLP_GROUPS):
        blk = gv[:, g * gd:(g + 1) * gd]
        ms = jnp.mean(blk * blk, axis=-1, keepdims=True)
        parts.append(blk * lax.rsqrt(ms + NORM_EPS))
    vn_ref[...] = (jnp.concatenate(parts, axis=1) * sg_ref[...]).astype(BF16)


def _mix_in(h, mods, g1, w_in_all, layer, qg, kg, cos, sin, gq, gk, sgg, n_ctx_tiles):
    batch, t, _ = h.shape
    tm = TOKEN_TILE
    tok = lambda n: pl.BlockSpec((None, tm, n), lambda b, i: (b, i, 0))
    tr = lambda n: pl.BlockSpec((None, n, tm), lambda b, i: (b, 0, i))
    return pl.pallas_call(
        _mix_in_kernel,
        grid=(batch, t // tm),
        in_specs=[tok(D_MODEL), _mod_spec(n_ctx_tiles, batch), _const_spec((1, D_MODEL)),
                  _layer_spec(w_in_all, layer), _const_spec((1, ATTN_WIDTH)), _const_spec((1, KV_WIDTH)),
                  pl.BlockSpec((tm, KV_WIDTH), lambda b, i: (i, 0)),
                  pl.BlockSpec((tm, KV_WIDTH), lambda b, i: (i, 0)),
                  _const_spec((ATTN_WIDTH, ATTN_WIDTH)), _const_spec((KV_WIDTH, KV_WIDTH)),
                  _const_spec((1, GMLP_WIDTH))],
        out_specs=[tr(ATTN_WIDTH), tok(KV_WIDTH), tr(N_KV_HEADS * V_ROWS), tok(GMLP_WIDTH), tok(GMLP_WIDTH)],
        out_shape=[jax.ShapeDtypeStruct((batch, ATTN_WIDTH, t), BF16),
                   jax.ShapeDtypeStruct((batch, t, KV_WIDTH), BF16),
                   jax.ShapeDtypeStruct((batch, N_KV_HEADS * V_ROWS, t), BF16),
                   jax.ShapeDtypeStruct((batch, t, GMLP_WIDTH), BF16),
                   jax.ShapeDtypeStruct((batch, t, GMLP_WIDTH), BF16)],
        compiler_params=_cparams(2),
        name="mix_in",
    )(h, mods, g1, w_in_all, qg, kg, cos, sin, gq, gk, sgg)


def _attn_kernel(safe_ref, qT_ref, k_ref, vT_ref, o_ref, qpad_ref, acc_ref, m_ref, *, tq, tk, tk_main,
                 n_ctx_tiles, ctx_len, t):
    i = pl.program_id(1)
    for kvh in range(N_KV_HEADS):
        qcat = jnp.concatenate(
            [qT_ref[(kvh * Q_PER_KV + g) * HEAD_DIM:(kvh * Q_PER_KV + g + 1) * HEAD_DIM, :]
             for g in range(Q_PER_KV)], axis=1)
        z = jnp.zeros_like(qcat)
        qpad_ref[kvh] = jnp.concatenate([qcat, z] if kvh == 0 else [z, qcat], axis=0)
    acc_ref[...] = jnp.zeros(acc_ref.shape, F32)
    is_latent = i >= n_ctx_tiles
    safe = safe_ref[0] != 0

    def plain_step(off, width):
        kblk = k_ref[pl.ds(off, width), :]
        for kvh in range(N_KV_HEADS):
            s = jnp.dot(kblk, qpad_ref[kvh], preferred_element_type=F32)
            p = jnp.exp2(s).astype(BF16)
            vblk = vT_ref[kvh * V_ROWS:(kvh + 1) * V_ROWS, pl.ds(off, width)]
            acc_ref[kvh] += jnp.dot(vblk, p, preferred_element_type=F32)

    @pl.when(safe)
    def _():
        plain_step(0, ctx_len)

        @pl.when(is_latent)
        def _():
            def body(j, carry):
                plain_step(pl.multiple_of(ctx_len + j * tk_main, LANES), tk_main)
                return carry
            lax.fori_loop(0, (t - ctx_len) // tk_main, body, 0, unroll=4)

    @pl.when(jnp.logical_not(safe))
    def _():
        m_ref[...] = jnp.full(m_ref.shape, -jnp.inf, F32)

        def body(j, carry):
            off = pl.multiple_of(j * tk, tk)
            kblk = k_ref[pl.ds(off, tk), :]
            for kvh in range(N_KV_HEADS):
                s = jnp.dot(kblk, qpad_ref[kvh], preferred_element_type=F32)
                m_old = m_ref[kvh]
                m_new = jnp.maximum(m_old, jnp.max(s, axis=0, keepdims=True))
                alpha = jnp.exp2(m_old - m_new)
                p = jnp.exp2(s - m_new).astype(BF16)
                vblk = vT_ref[kvh * V_ROWS:(kvh + 1) * V_ROWS, pl.ds(off, tk)]
                acc_ref[kvh] = acc_ref[kvh] * alpha + jnp.dot(vblk, p, preferred_element_type=F32)
                m_ref[kvh] = m_new
            return carry

        lax.fori_loop(0, jnp.where(is_latent, t // tk, ctx_len // tk), body, 0)

    for g in range(Q_PER_KV):
        og = jnp.concatenate(
            [acc_ref[kvh, 0:HEAD_DIM, g * tq:(g + 1) * tq]
             / acc_ref[kvh, HEAD_DIM:HEAD_DIM + 1, g * tq:(g + 1) * tq]
             for kvh in range(N_KV_HEADS)], axis=0)
        o_ref[:, g * KV_WIDTH:(g + 1) * KV_WIDTH] = og.T.astype(BF16)


def _attention(safe, qT, k, vT, n_ctx_tiles, ctx_len):
    batch, _, t = qT.shape
    tq = TOKEN_TILE
    tk = 256
    tk_main = 512
    assert ctx_len % tk == 0 and t % tk == 0 and (t - ctx_len) % tk_main == 0
    kern = functools.partial(_attn_kernel, tq=tq, tk=tk, tk_main=tk_main, n_ctx_tiles=n_ctx_tiles,
                             ctx_len=ctx_len, t=t)
    return pl.pallas_call(
        kern,
        grid=(batch, t // tq),
        in_specs=[pl.BlockSpec(memory_space=pltpu.SMEM),
                  pl.BlockSpec((None, ATTN_WIDTH, tq), lambda b, i: (b, 0, i)),
                  pl.BlockSpec((None, t, KV_WIDTH), lambda b, i: (b, 0, 0)),
                  pl.BlockSpec((None, N_KV_HEADS * V_ROWS, t), lambda b, i: (b, 0, 0))],
        out_specs=pl.BlockSpec((None, tq, ATTN_WIDTH), lambda b, i: (b, i, 0)),
        out_shape=jax.ShapeDtypeStruct((batch, t, ATTN_WIDTH), BF16),
        scratch_shapes=[pltpu.VMEM((N_KV_HEADS, KV_WIDTH, Q_PER_KV * tq), BF16),
                        pltpu.VMEM((N_KV_HEADS, V_ROWS, Q_PER_KV * tq), F32),
                        pltpu.VMEM((N_KV_HEADS, 1, Q_PER_KV * tq), F32)],
        compiler_params=_cparams(2),
        name="attention",
    )(safe, qT, k, vT)


FFN_SPLIT = 2


def _mix_project(att_ref, gu_ref, vn_ref, sw_ref, sb_ref, wo_ref):
    tm = att_ref.shape[0]
    gd = GMLP_WIDTH // GMLP_GROUPS
    chunks = []
    for c in range(tm // GMLP_CHUNK):
        rows = slice(c * GMLP_CHUNK, (c + 1) * GMLP_CHUNK)
        s = jnp.concatenate(
            [jnp.dot(sw_ref[g], vn_ref[rows, g * gd:(g + 1) * gd], preferred_element_type=F32)
             for g in range(GMLP_GROUPS)], axis=1) + sb_ref[...]
        chunks.append(gu_ref[rows, :].astype(F32) * s)
    sg = jnp.concatenate(chunks, axis=0).astype(BF16)
    return (jnp.dot(att_ref[...], wo_ref[0:ATTN_WIDTH, :], preferred_element_type=F32)
            + jnp.dot(sg, wo_ref[ATTN_WIDTH:, :], preferred_element_type=F32))


def _ssm_project(yf_ref, yb_ref, xs_ref, z_ref, dsk_ref, ng_ref, wo_ref):
    y = yf_ref[...].astype(F32) + yb_ref[...].astype(F32) + dsk_ref[...] * xs_ref[...].astype(F32)
    y = y * _silu(z_ref[...].astype(F32))
    gw = SSM_INNER // SSM_GROUPS
    parts = []
    for g in range(SSM_GROUPS):
        blk = y[:, g * gw:(g + 1) * gw]
        ms = jnp.mean(blk * blk, axis=-1, keepdims=True)
        parts.append(blk * lax.rsqrt(ms + NORM_EPS))
    yn = (jnp.concatenate(parts, axis=1) * ng_ref[...]).astype(BF16)
    return jnp.dot(yn, wo_ref[...], preferred_element_type=F32)


def _tail_kernel(*refs, project, n_tok, n_proj):
    tok_refs = refs[:n_tok]
    x_ref, mod_ref = refs[n_tok:n_tok + 2]
    proj_refs = refs[n_tok + 2:n_tok + 2 + n_proj]
    g2_ref, wi_ref, wf_ref, o_ref = refs[n_tok + 2 + n_proj:]
    mod = mod_ref[...]
    x = x_ref[...] + mod[:, 2 * D_MODEL:3 * D_MODEL] * project(*tok_refs, *proj_refs)
    h = _rms_mod(x, g2_ref[...], mod[:, 3 * D_MODEL:4 * D_MODEL], mod[:, 4 * D_MODEL:5 * D_MODEL]).astype(BF16)
    hc = FFN_HIDDEN // FFN_SPLIT
    acc = jnp.zeros(x.shape, F32)
    for c in range(FFN_SPLIT):
        g = jnp.dot(h, wi_ref[:, c * hc:(c + 1) * hc], preferred_element_type=F32)
        u = jnp.dot(h, wi_ref[:, FFN_HIDDEN + c * hc:FFN_HIDDEN + (c + 1) * hc], preferred_element_type=F32)
        a = (_silu(g) * u).astype(BF16)
        acc = acc + jnp.dot(a, wf_ref[c * hc:(c + 1) * hc, :], preferred_element_type=F32)
    o_ref[...] = x + mod[:, 5 * D_MODEL:6 * D_MODEL] * acc


def _layer_tail(project, name, tok_inputs, h, mods, proj_consts, ffn_consts, n_ctx_tiles, latent_only):
    batch, t, _ = h.shape
    tm = TOKEN_TILE
    skip = n_ctx_tiles if latent_only else 0
    tok = lambda n: pl.BlockSpec((None, tm, n), lambda b, i: (b, i + skip, 0))
    mod_spec = pl.BlockSpec((None, 1, 6 * D_MODEL),
                            lambda b, i: (jnp.where(i + skip < n_ctx_tiles, batch, b), 0, 0))
    def resident(c):
        if isinstance(c, tuple):
            return _layer_spec(*c, pipeline_mode=pl.Buffered(1))
        return pl.BlockSpec(c.shape, lambda b, i, nd=c.ndim: (0,) * nd, pipeline_mode=pl.Buffered(1))

    consts = list(proj_consts) + list(ffn_consts)
    arrays = [c[0] if isinstance(c, tuple) else c for c in consts]
    kern = functools.partial(_tail_kernel, project=project, n_tok=len(tok_inputs), n_proj=len(proj_consts))
    return pl.pallas_call(
        kern,
        grid=(batch, t // tm - skip),
        in_specs=([tok(a.shape[-1]) for a in tok_inputs] + [tok(D_MODEL), mod_spec]
                  + [resident(a) for a in consts]),
        out_specs=pl.BlockSpec((None, tm, D_MODEL), lambda b, i: (b, i, 0)),
        out_shape=jax.ShapeDtypeStruct((batch, t - skip * tm, D_MODEL), F32),
        compiler_params=_cparams(2),
        name=name,
    )(*tok_inputs, h, mods, *arrays)


def _ssm_in_kernel(xp_ref, x_ref, xn_ref, mod_ref, g1_ref, w_ref, wdt_ref, cw_ref, cb_ref, dtb_ref,
                   z_ref, xs_ref, bm_ref, cm_ref, dt_ref, *, n_tiles, n_ctx_tiles, col_block):
    i = pl.program_id(1)
    tm = x_ref.shape[0]
    halo = SUBLANES
    prev_ok = jnp.logical_and(i != 0, i != n_ctx_tiles)
    next_ok = jnp.logical_and(i != n_ctx_tiles - 1, i != n_tiles - 1)
    xcat = jnp.concatenate([x_ref[...], xn_ref[...], xp_ref[...]], axis=0)
    mod = mod_ref[...]
    h = _rms_mod(xcat, g1_ref[...], mod[:, 0:D_MODEL], mod[:, D_MODEL:2 * D_MODEL]).astype(BF16)
    hm = h[0:tm, :]
    z_ref[...] = jnp.dot(hm, w_ref[:, 0:SSM_INNER], preferred_element_type=F32).astype(BF16)
    dtr = jnp.dot(hm, wdt_ref[...], preferred_element_type=F32) + dtb_ref[...]
    dt_ref[...] = jnp.maximum(dtr, 0.0) + jnp.log1p(jnp.exp(-jnp.abs(dtr)))
    prev_f = prev_ok.astype(F32)
    next_f = next_ok.astype(F32)
    for blk in range(SSM_CONV_DIM // col_block):
        cols = slice(blk * col_block, (blk + 1) * col_block)
        p = jnp.dot(h, w_ref[:, SSM_INNER + blk * col_block:SSM_INNER + (blk + 1) * col_block],
                    preferred_element_type=F32)
        p = jnp.concatenate([p[0:tm, :], p[tm:tm + halo, :] * next_f, p[tm + halo:, :] * prev_f], axis=0)
        pm1 = pltpu.roll(p, 1, axis=0)[0:tm, :]
        pp1 = pltpu.roll(p, tm + 2 * halo - 1, axis=0)[0:tm, :]
        y = (cw_ref[0:1, cols] * pm1 + cw_ref[1:2, cols] * p[0:tm, :]
             + cw_ref[2:3, cols] * pp1 + cb_ref[:, cols])
        y = _silu(y).astype(BF16)
        c0 = blk * col_block
        if c0 < SSM_INNER:
            xs_ref[:, c0:c0 + col_block] = y
        elif c0 < SSM_INNER + SSM_BC_WIDTH:
            bm_ref[:, c0 - SSM_INNER:c0 - SSM_INNER + col_block] = y
        else:
            o0 = c0 - SSM_INNER - SSM_BC_WIDTH
            cm_ref[:, o0:o0 + col_block] = y


def _ssm_in(h, mods, g1, w_in_all, layer, wdt, cw, cb, dtb, n_ctx_tiles):
    batch, t, _ = h.shape
    tm = TOKEN_TILE
    n_tiles = t // tm
    per = tm // SUBLANES
    tok = lambda n: pl.BlockSpec((None, tm, n), lambda b, i: (b, i, 0))
    col_block = 512
    kern = functools.partial(_ssm_in_kernel, n_tiles=n_tiles, n_ctx_tiles=n_ctx_tiles, col_block=col_block)
    return pl.pallas_call(
        kern,
        grid=(batch, n_tiles),
        in_specs=[pl.BlockSpec((None, SUBLANES, D_MODEL), lambda b, i: (b, jnp.maximum(i * per - 1, 0), 0)),
                  tok(D_MODEL),
                  pl.BlockSpec((None, SUBLANES, D_MODEL),
                               lambda b, i: (b, jnp.minimum((i + 1) * per, t // SUBLANES - 1), 0)),
                  _mod_spec(n_ctx_tiles, batch), _const_spec((1, D_MODEL)),
                  _layer_spec(w_in_all, layer),
                  _const_spec((D_MODEL, LANES)), _const_spec((SUBLANES, SSM_CONV_DIM)),
                  _const_spec((1, SSM_CONV_DIM)), _const_spec((1, LANES))],
        out_specs=[tok(SSM_INNER), tok(SSM_INNER), tok(SSM_BC_WIDTH), tok(SSM_BC_WIDTH), tok(LANES)],
        out_shape=[jax.ShapeDtypeStruct((batch, t, SSM_INNER), BF16),
                   jax.ShapeDtypeStruct((batch, t, SSM_INNER), BF16),
                   jax.ShapeDtypeStruct((batch, t, SSM_BC_WIDTH), BF16),
                   jax.ShapeDtypeStruct((batch, t, SSM_BC_WIDTH), BF16),
                   jax.ShapeDtypeStruct((batch, t, LANES), F32)],
        compiler_params=_cparams(2),
        name="ssm_in",
    )(h, h, h, mods, g1, w_in_all, wdt, cw, cb, dtb)


def _ssd_kernel(xs_ref, bm_ref, cm_ref, dt_ref, a_ref, ex_ref, y_ref, st_ref, *, reverse, lane_base):
    L = SSM_CHUNK

    @pl.when(pl.program_id(1) == 0)
    def _():
        st_ref[...] = jnp.zeros(st_ref.shape, F32)

    dt = dt_ref[...]
    da = dt * a_ref[...]
    r = lax.broadcasted_iota(jnp.int32, (L, L), 0)
    c = lax.broadcasted_iota(jnp.int32, (L, L), 1)
    before = (c >= r) if reverse else (c <= r)
    tri = jnp.where(before, 1.0, 0.0).astype(BF16)
    hi = da.astype(BF16)
    r1 = da - hi.astype(F32)
    mid = r1.astype(BF16)
    lo = (r1 - mid.astype(F32)).astype(BF16)
    p3 = jnp.dot(tri, jnp.concatenate([hi, mid, lo], axis=1), preferred_element_type=F32)
    P = p3[:, 0:LANES] + p3[:, LANES:2 * LANES] + p3[:, 2 * LANES:3 * LANES]
    tot = jnp.sum(da, axis=0, keepdims=True)
    QT = (P - jnp.log(dt)).T

    def expand(v):
        return jnp.dot(jnp.concatenate(_split2(v), axis=1), ex_ref[...], preferred_element_type=F32)

    w_state_all = expand(dt * jnp.exp(tot - P))
    e_p_all = expand(jnp.exp(P))
    end = 0 if reverse else L - 1
    lane = lax.broadcasted_iota(jnp.int32, (1, LANES), 1)
    first_head = jnp.where(lane < SSM_HEAD_DIM, 1.0, 0.0).astype(BF16)
    second_head = jnp.where(lane >= SSM_HEAD_DIM, 1.0, 0.0).astype(BF16)
    heads_per_group = SSM_HEADS // SSM_GROUPS
    gw = heads_per_group * SSM_HEAD_DIM
    for g in range(SSM_GROUPS):
        e_p_x = e_p_all[:, g * gw:(g + 1) * gw]
        bg = bm_ref[:, g * SSM_STATE:(g + 1) * SSM_STATE]
        cg = cm_ref[:, g * SSM_STATE:(g + 1) * SSM_STATE]
        cb = lax.dot_general(cg, bg, (((1,), (1,)), ((), ())), preferred_element_type=F32)
        st = st_ref[g]
        y_off = jnp.dot(cg, st.astype(BF16), preferred_element_type=F32) * e_p_x
        ys = []
        for pair in range(heads_per_group // 2):
            ms = []
            for k in range(2):
                hl = lane_base + g * heads_per_group + 2 * pair + k
                seg = P[:, hl:hl + 1] - QT[hl:hl + 1, :]
                ms.append((cb * jnp.exp(jnp.where(before, seg, -jnp.inf))).astype(BF16))
            xp = xs_ref[:, g * gw + pair * LANES:g * gw + (pair + 1) * LANES]
            rhs = jnp.concatenate([xp * first_head, xp * second_head], axis=0)
            ys.append(jnp.dot(jnp.concatenate(ms, axis=1), rhs, preferred_element_type=F32))
        y_ref[:, g * gw:(g + 1) * gw] = (jnp.concatenate(ys, axis=1) + y_off).astype(BF16)
        xdd = xs_ref[:, g * gw:(g + 1) * gw] * w_state_all[:, g * gw:(g + 1) * gw].astype(BF16)
        bgT = bg.astype(F32).T.astype(BF16)
        st_ref[g] = st * e_p_x[end:end + 1, :] + jnp.dot(bgT, xdd, preferred_element_type=F32)


def _ssd(xs, bm, cm, dt, a_row, ex, reverse, n_ctx_chunks):
    batch, t, _ = xs.shape
    L = SSM_CHUNK
    nc = t // L
    if reverse:
        cmap = lambda i: jnp.where(i < n_ctx_chunks, n_ctx_chunks - 1 - i, nc - 1 + n_ctx_chunks - i)
    else:
        cmap = lambda i: i
    tok = lambda n: pl.BlockSpec((None, L, n), lambda b, i: (b, cmap(i), 0))
    kern = functools.partial(_ssd_kernel, reverse=reverse, lane_base=SSM_HEADS if reverse else 0)
    return pl.pallas_call(
        kern,
        grid=(batch, nc),
        in_specs=[tok(SSM_INNER), tok(SSM_BC_WIDTH), tok(SSM_BC_WIDTH), tok(LANES),
                  _const_spec((1, LANES)), _const_spec((2 * LANES, SSM_INNER))],
        out_specs=tok(SSM_INNER),
        out_shape=jax.ShapeDtypeStruct((batch, t, SSM_INNER), BF16),
        scratch_shapes=[pltpu.VMEM((SSM_GROUPS, SSM_STATE, SSM_INNER // SSM_GROUPS), F32)],
        compiler_params=_cparams(2),
        name="ssd_bwd" if reverse else "ssd_fwd",
    )(xs, bm, cm, dt, a_row, ex)


def _rope_tables(ctx_len, seq):
    tpos = jnp.arange(seq)
    row = (tpos // GRID_W).astype(F32)
    col = (tpos % GRID_W).astype(F32)
    n_freq = HEAD_DIM // 4
    inv = ROPE_THETA ** (-jnp.arange(n_freq, dtype=F32) / n_freq)
    ang = jnp.concatenate([row[:, None] * inv, col[:, None] * inv], axis=-1)
    cos = jnp.cos(ang)
    sin = jnp.sin(ang)
    cos64 = jnp.concatenate([cos, cos], axis=-1)
    sin64 = jnp.concatenate([-sin, sin], axis=-1)
    cos_t = jnp.concatenate([jnp.ones((ctx_len, HEAD_DIM), F32), cos64], axis=0)
    sin_t = jnp.concatenate([jnp.zeros((ctx_len, HEAD_DIM), F32), sin64], axis=0)
    return jnp.tile(cos_t, (1, N_KV_HEADS)), jnp.tile(sin_t, (1, N_KV_HEADS))


def _block_mean_matrix(n, group):
    idx = np.arange(n) // group
    return jnp.asarray((idx[:, None] == idx[None, :]).astype(np.float32) / group, dtype=BF16)


def kernel(x, c, ctx, c_ctx, ada_w, ada_b, norm1_g, norm2_g, ffn_w_in, ffn_w_out, mix_w_in, mix_w_out,
           q_norm_g, k_norm_g, sgu_norm_g, sgu_w, sgu_b, ssm_w_in, ssm_conv_w, ssm_conv_b, ssm_dt_bias,
           ssm_a_log, ssm_d, ssm_norm_g, ssm_w_out):
    batch, seq, _ = x.shape
    ctx_len = ctx.shape[1]
    depth = ada_w.shape[0]
    assert ctx_len % TOKEN_TILE == 0 and seq % TOKEN_TILE == 0 and batch < SUBLANES
    n_ctx_tiles = ctx_len // TOKEN_TILE
    n_ctx_chunks = ctx_len // SSM_CHUNK

    h = jnp.concatenate([ctx, x], axis=1)
    cond = jnp.zeros((SUBLANES, D_MODEL), F32).at[:batch].set(c).at[batch].set(c_ctx)
    mods = _adaln_all(cond, ada_w, ada_b)[:, :, None, :]

    cos_t, sin_t = _rope_tables(ctx_len, seq)
    gq = _block_mean_matrix(ATTN_WIDTH, HEAD_DIM)
    gk = _block_mean_matrix(KV_WIDTH, HEAD_DIM)
    perm = np.arange(ATTN_WIDTH).reshape(N_KV_HEADS, Q_PER_KV, HEAD_DIM).transpose(1, 0, 2).reshape(-1)
    ex_np = np.zeros((2, 2 * LANES, SSM_INNER), np.float32)
    for d in range(2):
        for hd in range(SSM_HEADS):
            for half in range(2):
                ex_np[d, half * LANES + d * SSM_HEADS + hd, hd * SSM_HEAD_DIM:(hd + 1) * SSM_HEAD_DIM] = 1.0
    ex = jnp.asarray(ex_np, dtype=BF16)

    ffn_w_in_b = ffn_w_in.astype(BF16)
    ffn_w_out_b = ffn_w_out.astype(BF16)
    mix_w_in_b = mix_w_in.astype(BF16)
    ssm_w_in_b = ssm_w_in.astype(BF16)
    ssm_w_out_b = ssm_w_out.astype(BF16)
    norm2_rows = norm2_g[:, None, :]
    for i in range(depth):
        j = i // 2
        m = mods[i]
        g1 = norm1_g[i][None, :]
        last = i == depth - 1
        ffn_consts = [(norm2_rows, i), (ffn_w_in_b, i), (ffn_w_out_b, i)]
        if i % 2 == 0:
            qT, k, vT, gu, vn = _mix_in(
                h, m, g1, mix_w_in_b, j, jnp.tile(q_norm_g[j], N_Q_HEADS)[None, :],
                jnp.tile(k_norm_g[j], N_KV_HEADS)[None, :], cos_t, sin_t, gq, gk,
                sgu_norm_g[j][None, :], n_ctx_tiles)
            score_bound = (1.02 * HEAD_DIM ** 0.5 * LOG2E) * jnp.max(jnp.abs(q_norm_g[j])) * jnp.max(jnp.abs(k_norm_g[j]))
            safe = (score_bound <= SAFE_LOG2_BOUND).astype(jnp.int32).reshape(1)
            att = _attention(safe, qT, k, vT, n_ctx_tiles, ctx_len)
            w_out = mix_w_out[j]
            w_out = jnp.concatenate([w_out[:ATTN_WIDTH][perm], w_out[ATTN_WIDTH:]], axis=0).astype(BF16)
            sb = jnp.repeat(sgu_b[j].T, GMLP_WIDTH // GMLP_GROUPS, axis=1)
            h = _layer_tail(_mix_project, "mix_tail", [att, gu, vn], h, m,
                            [sgu_w[j].astype(BF16), sb, w_out], ffn_consts, n_ctx_tiles, last)
        else:
            n_dt = 2 * SSM_HEADS
            wdt = jnp.zeros((D_MODEL, LANES), BF16).at[:, :n_dt].set(ssm_w_in_b[j, :, SSM_INNER + SSM_CONV_DIM:])
            dtb = jnp.zeros((1, LANES), F32).at[0, :n_dt].set(ssm_dt_bias[j].reshape(-1))
            cw = jnp.zeros((SUBLANES, SSM_CONV_DIM), F32).at[:ssm_conv_w.shape[1]].set(ssm_conv_w[j])
            z, xs, bm, cm, dt = _ssm_in(h, m, g1, ssm_w_in_b, j, wdt, cw, ssm_conv_b[j][None, :], dtb, n_ctx_tiles)
            a = -jnp.exp(ssm_a_log[j].astype(F32))
            ys = []
            for d in range(2):
                a_row = jnp.zeros((1, LANES), F32).at[0, d * SSM_HEADS:(d + 1) * SSM_HEADS].set(a[d])
                ys.append(_ssd(xs, bm, cm, dt, a_row, ex[d], d == 1, n_ctx_chunks))
            dsk = jnp.repeat(ssm_d[j], SSM_HEAD_DIM)[None, :]
            h = _layer_tail(_ssm_project, "ssm_tail", [ys[0], ys[1], xs, z], h, m,
                            [dsk, ssm_norm_g[j][None, :], (ssm_w_out_b, j)], ffn_consts,
                            n_ctx_tiles, last)
    return h
```

```python
import functools
import math

import jax
import jax.numpy as jnp
import numpy as np
from jax import lax
from jax.experimental import pallas as pl
from jax.experimental.pallas import tpu as pltpu

F32 = jnp.float32
BF16 = jnp.bfloat16

D_MODEL = 1024
GRID_W = 64
HEAD_DIM = 64
ATTN_WIDTH = D_MODEL // 2
N_Q_HEADS = ATTN_WIDTH // HEAD_DIM
N_KV_HEADS = 2
Q_PER_KV = N_Q_HEADS // N_KV_HEADS
KV_WIDTH = N_KV_HEADS * HEAD_DIM
ROPE_THETA = 10000.0
GMLP_WIDTH = D_MODEL // 2
GMLP_CHUNK = 128
GMLP_GROUPS = 4
MIX_IN = ATTN_WIDTH + 2 * KV_WIDTH + 2 * GMLP_WIDTH
SSM_INNER = 2 * D_MODEL
SSM_HEAD_DIM = 64
SSM_HEADS = SSM_INNER // SSM_HEAD_DIM
SSM_GROUPS = 4
SSM_STATE = 128
SSM_CHUNK = 128
SSM_BC_WIDTH = SSM_GROUPS * SSM_STATE
SSM_CONV_DIM = SSM_INNER + 2 * SSM_BC_WIDTH
FFN_HIDDEN = -(-8 * D_MODEL // (3 * 256)) * 256
NORM_EPS = 1e-6

LANES = 128
SUBLANES = 8
TOKEN_TILE = 256
VMEM_LIMIT = 56 * 1024 * 1024
LOG2E = 1.4426950408889634
BF16_ROWS = 16
V_ROWS = HEAD_DIM + BF16_ROWS
SAFE_LOG2_BOUND = 60.0


def _cparams(n_axes):
    return pltpu.CompilerParams(dimension_semantics=("arbitrary",) * n_axes,
                                vmem_limit_bytes=VMEM_LIMIT)


def _silu(x):
    return x / (1.0 + jnp.exp(-x))


def _gelu_tanh(x):
    c = math.sqrt(2.0 / math.pi)
    return x * (0.5 * (1.0 + jnp.tanh(c * (x + 0.044715 * (x * x * x)))))


def _rms_mod(x, g, shift, scale):
    ms = jnp.mean(x * x, axis=-1, keepdims=True)
    return (x * lax.rsqrt(ms + NORM_EPS)) * (g * (1.0 + scale)) + shift


def _split2(x):
    hi = x.astype(BF16)
    lo = (x - hi.astype(F32)).astype(BF16)
    return hi, lo


def _group_mean(x2, gmat):
    hi, lo = _split2(x2)
    return (jnp.dot(hi, gmat, preferred_element_type=F32)
            + jnp.dot(lo, gmat, preferred_element_type=F32))


def _rope(x, cos, sin_signed):
    n = x.shape[-1]
    half = HEAD_DIM // 2
    fwd = pltpu.roll(x, n - half, axis=1)
    bwd = pltpu.roll(x, half, axis=1)
    lane = lax.broadcasted_iota(jnp.int32, x.shape, 1)
    partner = jnp.where((lane % HEAD_DIM) < half, fwd, bwd)
    return x * cos + partner * sin_signed


def _adaln_kernel(c_ref, w_ref, b_ref, o_ref):
    s = _silu(c_ref[...]).astype(BF16)
    o_ref[...] = jnp.dot(s, w_ref[...].astype(BF16), preferred_element_type=F32) + b_ref[...]


def _adaln_all(cond, ada_w, ada_b):
    depth = ada_w.shape[0]
    tn = 1536
    return pl.pallas_call(
        _adaln_kernel,
        grid=(depth, 6 * D_MODEL // tn),
        in_specs=[pl.BlockSpec((SUBLANES, D_MODEL), lambda l, j: (0, 0)),
                  pl.BlockSpec((None, D_MODEL, tn), lambda l, j: (l, 0, j)),
                  pl.BlockSpec((None, 1, tn), lambda l, j: (l, 0, j))],
        out_specs=pl.BlockSpec((None, SUBLANES, tn), lambda l, j: (l, 0, j)),
        out_shape=jax.ShapeDtypeStruct((depth, SUBLANES, 6 * D_MODEL), F32),
        compiler_params=_cparams(2),
        name="adaln",
    )(cond, ada_w, ada_b.reshape(depth, 1, 6 * D_MODEL))


def _mod_spec(n_ctx_tiles, batch):
    return pl.BlockSpec((None, 1, 6 * D_MODEL),
                        lambda b, i: (jnp.where(i < n_ctx_tiles, batch, b), 0, 0))


def _const_spec(shape):
    nd = len(shape)
    return pl.BlockSpec(shape, lambda b, i: (0,) * nd)


def _layer_spec(stacked, layer, **kwargs):
    idx = (layer,) + (0,) * (stacked.ndim - 1)
    return pl.BlockSpec((None,) + stacked.shape[1:], lambda b, i: idx, **kwargs)


def _mix_in_kernel(x_ref, mod_ref, g1_ref, w_ref, qg_ref, kg_ref, cos_ref, sin_ref, gq_ref, gk_ref,
                   sg_ref, qT_ref, k_ref, vT_ref, gu_ref, vn_ref):
    mod = mod_ref[...]
    h = _rms_mod(x_ref[...], g1_ref[...], mod[:, 0:D_MODEL], mod[:, D_MODEL:2 * D_MODEL]).astype(BF16)
    p = jnp.dot(h, w_ref[...], preferred_element_type=F32)
    q = p[:, 0:ATTN_WIDTH]
    k = p[:, ATTN_WIDTH:ATTN_WIDTH + KV_WIDTH]
    v = p[:, ATTN_WIDTH + KV_WIDTH:ATTN_WIDTH + 2 * KV_WIDTH]
    gm = p[:, ATTN_WIDTH + 2 * KV_WIDTH:]
    cos = cos_ref[...]
    sin = sin_ref[...]
    q = q * lax.rsqrt(_group_mean(q * q, gq_ref[...]) + NORM_EPS) * qg_ref[...]
    k = k * lax.rsqrt(_group_mean(k * k, gk_ref[...]) + NORM_EPS) * kg_ref[...]
    reps = ATTN_WIDTH // KV_WIDTH
    q = _rope(q, jnp.concatenate([cos] * reps, axis=1), jnp.concatenate([sin] * reps, axis=1))
    k = _rope(k, cos, sin)
    q = q * (HEAD_DIM ** -0.5 * LOG2E)
    qT_ref[...] = q.T.astype(BF16)
    k_ref[...] = k.astype(BF16)
    vT = v.T
    ones = jnp.ones((BF16_ROWS, vT.shape[1]), F32)
    vT_ref[...] = jnp.concatenate(
        [blk for kvh in range(N_KV_HEADS) for blk in (vT[kvh * HEAD_DIM:(kvh + 1) * HEAD_DIM, :], ones)],
        axis=0).astype(BF16)
    ge = _gelu_tanh(gm)
    gu_ref[...] = ge[:, 0:GMLP_WIDTH].astype(BF16)
    gv = ge[:, GMLP_WIDTH:]
    gd = GMLP_WIDTH // GMLP_GROUPS
    parts = []
    for g in range(GMLP_GROUPS):
        blk = gv[:, g * gd:(g + 1) * gd]
        ms = jnp.mean(blk * blk, axis=-1, keepdims=True)
        parts.append(blk * lax.rsqrt(ms + NORM_EPS))
    vn_ref[...] = (jnp.concatenate(parts, axis=1) * sg_ref[...]).astype(BF16)


def _mix_in(h, mods, g1, w_in_all, layer, qg, kg, cos, sin, gq, gk, sgg, n_ctx_tiles):
    batch, t, _ = h.shape
    tm = TOKEN_TILE
    tok = lambda n: pl.BlockSpec((None, tm, n), lambda b, i: (b, i, 0))
    tr = lambda n: pl.BlockSpec((None, n, tm), lambda b, i: (b, 0, i))
    return pl.pallas_call(
        _mix_in_kernel,
        grid=(batch, t // tm),
        in_specs=[tok(D_MODEL), _mod_spec(n_ctx_tiles, batch), _const_spec((1, D_MODEL)),
                  _layer_spec(w_in_all, layer), _const_spec((1, ATTN_WIDTH)), _const_spec((1, KV_WIDTH)),
                  pl.BlockSpec((tm, KV_WIDTH), lambda b, i: (i, 0)),
                  pl.BlockSpec((tm, KV_WIDTH), lambda b, i: (i, 0)),
                  _const_spec((ATTN_WIDTH, ATTN_WIDTH)), _const_spec((KV_WIDTH, KV_WIDTH)),
                  _const_spec((1, GMLP_WIDTH))],
        out_specs=[tr(ATTN_WIDTH), tok(KV_WIDTH), tr(N_KV_HEADS * V_ROWS), tok(GMLP_WIDTH), tok(GMLP_WIDTH)],
        out_shape=[jax.ShapeDtypeStruct((batch, ATTN_WIDTH, t), BF16),
                   jax.ShapeDtypeStruct((batch, t, KV_WIDTH), BF16),
                   jax.ShapeDtypeStruct((batch, N_KV_HEADS * V_ROWS, t), BF16),
                   jax.ShapeDtypeStruct((batch, t, GMLP_WIDTH), BF16),
                   jax.ShapeDtypeStruct((batch, t, GMLP_WIDTH), BF16)],
        compiler_params=_cparams(2),
        name="mix_in",
    )(h, mods, g1, w_in_all, qg, kg, cos, sin, gq, gk, sgg)


def _attn_kernel(safe_ref, qT_ref, k_ref, vT_ref, o_ref, qpad_ref, acc_ref, m_ref, *, tq, tk, tk_main,
                 n_ctx_tiles, ctx_len, t):
    i = pl.program_id(1)
    for kvh in range(N_KV_HEADS):
        qcat = jnp.concatenate(
            [qT_ref[(kvh * Q_PER_KV + g) * HEAD_DIM:(kvh * Q_PER_KV + g + 1) * HEAD_DIM, :]
             for g in range(Q_PER_KV)], axis=1)
        z = jnp.zeros_like(qcat)
        qpad_ref[kvh] = jnp.concatenate([qcat, z] if kvh == 0 else [z, qcat], axis=0)
    acc_ref[...] = jnp.zeros(acc_ref.shape, F32)
    is_latent = i >= n_ctx_tiles
    safe = safe_ref[0] != 0

    def plain_step(off, width):
        kblk = k_ref[pl.ds(off, width), :]
        for kvh in range(N_KV_HEADS):
            s = jnp.dot(kblk, qpad_ref[kvh], preferred_element_type=F32)
            p = jnp.exp2(s).astype(BF16)
            vblk = vT_ref[kvh * V_ROWS:(kvh + 1) * V_ROWS, pl.ds(off, width)]
            acc_ref[kvh] += jnp.dot(vblk, p, preferred_element_type=F32)

    @pl.when(safe)
    def _():
        plain_step(0, ctx_len)

        @pl.when(is_latent)
        def _():
            def body(j, carry):
                plain_step(pl.multiple_of(ctx_len + j * tk_main, LANES), tk_main)
                return carry
            lax.fori_loop(0, (t - ctx_len) // tk_main, body, 0, unroll=4)

    @pl.when(jnp.logical_not(safe))
    def _():
        m_ref[...] = jnp.full(m_ref.shape, -jnp.inf, F32)

        def body(j, carry):
            off = pl.multiple_of(j * tk, tk)
            kblk = k_ref[pl.ds(off, tk), :]
            for kvh in range(N_KV_HEADS):
                s = jnp.dot(kblk, qpad_ref[kvh], preferred_element_type=F32)
                m_old = m_ref[kvh]
                m_new = jnp.maximum(m_old, jnp.max(s, axis=0, keepdims=True))
                alpha = jnp.exp2(m_old - m_new)
                p = jnp.exp2(s - m_new).astype(BF16)
                vblk = vT_ref[kvh * V_ROWS:(kvh + 1) * V_ROWS, pl.ds(off, tk)]
                acc_ref[kvh] = acc_ref[kvh] * alpha + jnp.dot(vblk, p, preferred_element_type=F32)
                m_ref[kvh] = m_new
            return carry

        lax.fori_loop(0, jnp.where(is_latent, t // tk, ctx_len // tk), body, 0)

    for g in range(Q_PER_KV):
        og = jnp.concatenate(
            [acc_ref[kvh, 0:HEAD_DIM, g * tq:(g + 1) * tq]
             / acc_ref[kvh, HEAD_DIM:HEAD_DIM + 1, g * tq:(g + 1) * tq]
             for kvh in range(N_KV_HEADS)], axis=0)
        o_ref[:, g * KV_WIDTH:(g + 1) * KV_WIDTH] = og.T.astype(BF16)


def _attention(safe, qT, k, vT, n_ctx_tiles, ctx_len):
    batch, _, t = qT.shape
    tq = TOKEN_TILE
    tk = 256
    tk_main = 1024
    assert ctx_len % tk == 0 and t % tk == 0 and (t - ctx_len) % tk_main == 0
    kern = functools.partial(_attn_kernel, tq=tq, tk=tk, tk_main=tk_main, n_ctx_tiles=n_ctx_tiles,
                             ctx_len=ctx_len, t=t)
    return pl.pallas_call(
        kern,
        grid=(batch, t // tq),
        in_specs=[pl.BlockSpec(memory_space=pltpu.SMEM),
                  pl.BlockSpec((None, ATTN_WIDTH, tq), lambda b, i: (b, 0, i)),
                  pl.BlockSpec((None, t, KV_WIDTH), lambda b, i: (b, 0, 0)),
                  pl.BlockSpec((None, N_KV_HEADS * V_ROWS, t), lambda b, i: (b, 0, 0))],
        out_specs=pl.BlockSpec((None, tq, ATTN_WIDTH), lambda b, i: (b, i, 0)),
        out_shape=jax.ShapeDtypeStruct((batch, t, ATTN_WIDTH), BF16),
        scratch_shapes=[pltpu.VMEM((N_KV_HEADS, KV_WIDTH, Q_PER_KV * tq), BF16),
                        pltpu.VMEM((N_KV_HEADS, V_ROWS, Q_PER_KV * tq), F32),
                        pltpu.VMEM((N_KV_HEADS, 1, Q_PER_KV * tq), F32)],
        compiler_params=_cparams(2),
        name="attention",
    )(safe, qT, k, vT)


FFN_SPLIT = 2


def _mix_project(att_ref, gu_ref, vn_ref, sw_ref, sb_ref, wo_ref):
    tm = att_ref.shape[0]
    gd = GMLP_WIDTH // GMLP_GROUPS
    chunks = []
    for c in range(tm // GMLP_CHUNK):
        rows = slice(c * GMLP_CHUNK, (c + 1) * GMLP_CHUNK)
        s = jnp.concatenate(
            [jnp.dot(sw_ref[g], vn_ref[rows, g * gd:(g + 1) * gd], preferred_element_type=F32)
             for g in range(GMLP_GROUPS)], axis=1) + sb_ref[...]
        chunks.append(gu_ref[rows, :].astype(F32) * s)
    sg = jnp.concatenate(chunks, axis=0).astype(BF16)
    return (jnp.dot(att_ref[...], wo_ref[0:ATTN_WIDTH, :], preferred_element_type=F32)
            + jnp.dot(sg, wo_ref[ATTN_WIDTH:, :], preferred_element_type=F32))


def _ssm_project(yf_ref, yb_ref, xs_ref, z_ref, dsk_ref, ng_ref, wo_ref):
    gw = SSM_INNER // SSM_GROUPS
    out = None
    for g in range(SSM_GROUPS):
        cols = slice(g * gw, (g + 1) * gw)
        y = (yf_ref[:, cols].astype(F32) + yb_ref[:, cols].astype(F32)
             + dsk_ref[:, cols] * xs_ref[:, cols].astype(F32))
        y = y * _silu(z_ref[:, cols].astype(F32))
        ms = jnp.mean(y * y, axis=-1, keepdims=True)
        yn = (y * lax.rsqrt(ms + NORM_EPS) * ng_ref[:, cols]).astype(BF16)
        part = jnp.dot(yn, wo_ref[cols, :], preferred_element_type=F32)
        out = part if out is None else out + part
    return out


def _tail_kernel(*refs, project, n_tok, n_proj):
    tok_refs = refs[:n_tok]
    x_ref, mod_ref = refs[n_tok:n_tok + 2]
    proj_refs = refs[n_tok + 2:n_tok + 2 + n_proj]
    g2_ref, wi_ref, wf_ref, o_ref = refs[n_tok + 2 + n_proj:]
    mod = mod_ref[...]
    x = x_ref[...] + mod[:, 2 * D_MODEL:3 * D_MODEL] * project(*tok_refs, *proj_refs)
    h = _rms_mod(x, g2_ref[...], mod[:, 3 * D_MODEL:4 * D_MODEL], mod[:, 4 * D_MODEL:5 * D_MODEL]).astype(BF16)
    hc = FFN_HIDDEN // FFN_SPLIT
    acc = jnp.zeros(x.shape, F32)
    for c in range(FFN_SPLIT):
        g = jnp.dot(h, wi_ref[:, c * hc:(c + 1) * hc], preferred_element_type=F32)
        u = jnp.dot(h, wi_ref[:, FFN_HIDDEN + c * hc:FFN_HIDDEN + (c + 1) * hc], preferred_element_type=F32)
        a = (_silu(g) * u).astype(BF16)
        acc = acc + jnp.dot(a, wf_ref[c * hc:(c + 1) * hc, :], preferred_element_type=F32)
    o_ref[...] = x + mod[:, 5 * D_MODEL:6 * D_MODEL] * acc


def _layer_tail(project, name, tok_inputs, h, mods, proj_consts, ffn_consts, n_ctx_tiles, latent_only):
    batch, t, _ = h.shape
    tm = TOKEN_TILE
    skip = n_ctx_tiles if latent_only else 0
    tok = lambda n: pl.BlockSpec((None, tm, n), lambda b, i: (b, i + skip, 0))
    mod_spec = pl.BlockSpec((None, 1, 6 * D_MODEL),
                            lambda b, i: (jnp.where(i + skip < n_ctx_tiles, batch, b), 0, 0))
    def resident(c):
        if isinstance(c, tuple):
            return _layer_spec(*c, pipeline_mode=pl.Buffered(1))
        return pl.BlockSpec(c.shape, lambda b, i, nd=c.ndim: (0,) * nd, pipeline_mode=pl.Buffered(1))

    consts = list(proj_consts) + list(ffn_consts)
    arrays = [c[0] if isinstance(c, tuple) else c for c in consts]
    kern = functools.partial(_tail_kernel, project=project, n_tok=len(tok_inputs), n_proj=len(proj_consts))
    return pl.pallas_call(
        kern,
        grid=(batch, t // tm - skip),
        in_specs=([tok(a.shape[-1]) for a in tok_inputs] + [tok(D_MODEL), mod_spec]
                  + [resident(a) for a in consts]),
        out_specs=pl.BlockSpec((None, tm, D_MODEL), lambda b, i: (b, i, 0)),
        out_shape=jax.ShapeDtypeStruct((batch, t - skip * tm, D_MODEL), F32),
        compiler_params=_cparams(2),
        name=name,
    )(*tok_inputs, h, mods, *arrays)


def _ssm_in_kernel(xp_ref, x_ref, xn_ref, mod_ref, g1_ref, w_ref, wdt_ref, cw_ref, cb_ref, dtb_ref,
                   z_ref, xs_ref, bm_ref, cm_ref, dt_ref, *, n_tiles, n_ctx_tiles, col_block):
    i = pl.program_id(1)
    tm = x_ref.shape[0]
    halo = SUBLANES
    prev_ok = jnp.logical_and(i != 0, i != n_ctx_tiles)
    next_ok = jnp.logical_and(i != n_ctx_tiles - 1, i != n_tiles - 1)
    xcat = jnp.concatenate([x_ref[...], xn_ref[...], xp_ref[...]], axis=0)
    mod = mod_ref[...]
    h = _rms_mod(xcat, g1_ref[...], mod[:, 0:D_MODEL], mod[:, D_MODEL:2 * D_MODEL]).astype(BF16)
    hm = h[0:tm, :]
    z_ref[...] = jnp.dot(hm, w_ref[:, 0:SSM_INNER], preferred_element_type=F32).astype(BF16)
    dtr = jnp.dot(hm, wdt_ref[...], preferred_element_type=F32) + dtb_ref[...]
    dt_ref[...] = jnp.maximum(dtr, 0.0) + jnp.log1p(jnp.exp(-jnp.abs(dtr)))
    prev_f = prev_ok.astype(F32)
    next_f = next_ok.astype(F32)
    for blk in range(SSM_CONV_DIM // col_block):
        cols = slice(blk * col_block, (blk + 1) * col_block)
        p = jnp.dot(h, w_ref[:, SSM_INNER + blk * col_block:SSM_INNER + (blk + 1) * col_block],
                    preferred_element_type=F32)
        p = jnp.concatenate([p[0:tm, :], p[tm:tm + halo, :] * next_f, p[tm + halo:, :] * prev_f], axis=0)
        pm1 = pltpu.roll(p, 1, axis=0)[0:tm, :]
        pp1 = pltpu.roll(p, tm + 2 * halo - 1, axis=0)[0:tm, :]
        y = (cw_ref[0:1, cols] * pm1 + cw_ref[1:2, cols] * p[0:tm, :]
             + cw_ref[2:3, cols] * pp1 + cb_ref[:, cols])
        y = _silu(y).astype(BF16)
        c0 = blk * col_block
        if c0 < SSM_INNER:
            xs_ref[:, c0:c0 + col_block] = y
        elif c0 < SSM_INNER + SSM_BC_WIDTH:
            bm_ref[:, c0 - SSM_INNER:c0 - SSM_INNER + col_block] = y
        else:
            o0 = c0 - SSM_INNER - SSM_BC_WIDTH
            cm_ref[:, o0:o0 + col_block] = y


def _ssm_in(h, mods, g1, w_in_all, layer, wdt, cw, cb, dtb, n_ctx_tiles):
    batch, t, _ = h.shape
    tm = TOKEN_TILE
    n_tiles = t // tm
    per = tm // SUBLANES
    tok = lambda n: pl.BlockSpec((None, tm, n), lambda b, i: (b, i, 0))
    col_block = 512
    kern = functools.partial(_ssm_in_kernel, n_tiles=n_tiles, n_ctx_tiles=n_ctx_tiles, col_block=col_block)
    return pl.pallas_call(
        kern,
        grid=(batch, n_tiles),
        in_specs=[pl.BlockSpec((None, SUBLANES, D_MODEL), lambda b, i: (b, jnp.maximum(i * per - 1, 0), 0)),
                  tok(D_MODEL),
                  pl.BlockSpec((None, SUBLANES, D_MODEL),
                               lambda b, i: (b, jnp.minimum((i + 1) * per, t // SUBLANES - 1), 0)),
                  _mod_spec(n_ctx_tiles, batch), _const_spec((1, D_MODEL)),
                  _layer_spec(w_in_all, layer),
                  _const_spec((D_MODEL, LANES)), _const_spec((SUBLANES, SSM_CONV_DIM)),
                  _const_spec((1, SSM_CONV_DIM)), _const_spec((1, LANES))],
        out_specs=[tok(SSM_INNER), tok(SSM_INNER), tok(SSM_BC_WIDTH), tok(SSM_BC_WIDTH), tok(LANES)],
        out_shape=[jax.ShapeDtypeStruct((batch, t, SSM_INNER), BF16),
                   jax.ShapeDtypeStruct((batch, t, SSM_INNER), BF16),
                   jax.ShapeDtypeStruct((batch, t, SSM_BC_WIDTH), BF16),
                   jax.ShapeDtypeStruct((batch, t, SSM_BC_WIDTH), BF16),
                   jax.ShapeDtypeStruct((batch, t, LANES), F32)],
        compiler_params=_cparams(2),
        name="ssm_in",
    )(h, h, h, mods, g1, w_in_all, wdt, cw, cb, dtb)


def _ssd_chunk(xs_ref, bm_ref, cm_ref, dt_ref, a_ref, ex_ref, y_ref, st_ref, *, reverse, lane_base):
    L = SSM_CHUNK

    @pl.when(pl.program_id(1) == 0)
    def _():
        st_ref[...] = jnp.zeros(st_ref.shape, F32)

    dt = dt_ref[...]
    da = dt * a_ref[...]
    r = lax.broadcasted_iota(jnp.int32, (L, L), 0)
    c = lax.broadcasted_iota(jnp.int32, (L, L), 1)
    before = (c >= r) if reverse else (c <= r)
    tri = jnp.where(before, 1.0, 0.0).astype(BF16)
    hi = da.astype(BF16)
    r1 = da - hi.astype(F32)
    mid = r1.astype(BF16)
    lo = (r1 - mid.astype(F32)).astype(BF16)
    p3 = jnp.dot(tri, jnp.concatenate([hi, mid, lo], axis=1), preferred_element_type=F32)
    P = p3[:, 0:LANES] + p3[:, LANES:2 * LANES] + p3[:, 2 * LANES:3 * LANES]
    tot = jnp.sum(da, axis=0, keepdims=True)
    QT = (P - jnp.log(dt)).T

    def expand(v):
        return jnp.dot(jnp.concatenate(_split2(v), axis=1), ex_ref[...], preferred_element_type=F32)

    w_state_all = expand(dt * jnp.exp(tot - P))
    e_p_all = expand(jnp.exp(P))
    end = 0 if reverse else L - 1
    lane = lax.broadcasted_iota(jnp.int32, (1, LANES), 1)
    first_head = jnp.where(lane < SSM_HEAD_DIM, 1.0, 0.0).astype(BF16)
    second_head = jnp.where(lane >= SSM_HEAD_DIM, 1.0, 0.0).astype(BF16)
    heads_per_group = SSM_HEADS // SSM_GROUPS
    gw = heads_per_group * SSM_HEAD_DIM
    for g in range(SSM_GROUPS):
        e_p_x = e_p_all[:, g * gw:(g + 1) * gw]
        bg = bm_ref[:, g * SSM_STATE:(g + 1) * SSM_STATE]
        cg = cm_ref[:, g * SSM_STATE:(g + 1) * SSM_STATE]
        cb = lax.dot_general(cg, bg, (((1,), (1,)), ((), ())), preferred_element_type=F32)
        st = st_ref[g]
        y_off = jnp.dot(cg, st.astype(BF16), preferred_element_type=F32) * e_p_x
        ys = []
        for pair in range(heads_per_group // 2):
            ms = []
            for k in range(2):
                hl = lane_base + g * heads_per_group + 2 * pair + k
                seg = P[:, hl:hl + 1] - QT[hl:hl + 1, :]
                ms.append((cb * jnp.exp(jnp.where(before, seg, -jnp.inf))).astype(BF16))
            xp = xs_ref[:, g * gw + pair * LANES:g * gw + (pair + 1) * LANES]
            rhs = jnp.concatenate([xp * first_head, xp * second_head], axis=0)
            ys.append(jnp.dot(jnp.concatenate(ms, axis=1), rhs, preferred_element_type=F32))
        y_ref[:, g * gw:(g + 1) * gw] = (jnp.concatenate(ys, axis=1) + y_off).astype(BF16)
        xdd = xs_ref[:, g * gw:(g + 1) * gw] * w_state_all[:, g * gw:(g + 1) * gw].astype(BF16)
        bgT = bg.astype(F32).T.astype(BF16)
        st_ref[g] = st * e_p_x[end:end + 1, :] + jnp.dot(bgT, xdd, preferred_element_type=F32)


def _ssd_pair_kernel(xf_ref, bf_ref, cf_ref, dtf_ref, xb_ref, bb_ref, cb_ref, dtb_ref, a_ref, ex_ref,
                     yf_ref, yb_ref, stf_ref, stb_ref):
    _ssd_chunk(xf_ref, bf_ref, cf_ref, dtf_ref, a_ref.at[0], ex_ref.at[0], yf_ref, stf_ref,
               reverse=False, lane_base=0)
    _ssd_chunk(xb_ref, bb_ref, cb_ref, dtb_ref, a_ref.at[1], ex_ref.at[1], yb_ref, stb_ref,
               reverse=True, lane_base=SSM_HEADS)


def _ssd(xs, bm, cm, dt, a_rows, ex, n_ctx_chunks):
    batch, t, _ = xs.shape
    L = SSM_CHUNK
    nc = t // L
    fwd = lambda i: i
    bwd = lambda i: jnp.where(i < n_ctx_chunks, n_ctx_chunks - 1 - i, nc - 1 + n_ctx_chunks - i)
    tok = lambda n, cmap: pl.BlockSpec((None, L, n), lambda b, i: (b, cmap(i), 0))
    chunk_specs = lambda cmap: [tok(SSM_INNER, cmap), tok(SSM_BC_WIDTH, cmap), tok(SSM_BC_WIDTH, cmap),
                                tok(LANES, cmap)]
    state = pltpu.VMEM((SSM_GROUPS, SSM_STATE, SSM_INNER // SSM_GROUPS), F32)
    y_shape = jax.ShapeDtypeStruct((batch, t, SSM_INNER), BF16)
    return pl.pallas_call(
        _ssd_pair_kernel,
        grid=(batch, nc),
        in_specs=(chunk_specs(fwd) + chunk_specs(bwd)
                  + [_const_spec((2, 1, LANES)), _const_spec((2, 2 * LANES, SSM_INNER))]),
        out_specs=[tok(SSM_INNER, fwd), tok(SSM_INNER, bwd)],
        out_shape=[y_shape, y_shape],
        scratch_shapes=[state, state],
        compiler_params=_cparams(2),
        name="ssd",
    )(xs, bm, cm, dt, xs, bm, cm, dt, a_rows, ex)


def _rope_tables(ctx_len, seq):
    tpos = jnp.arange(seq)
    row = (tpos // GRID_W).astype(F32)
    col = (tpos % GRID_W).astype(F32)
    n_freq = HEAD_DIM // 4
    inv = ROPE_THETA ** (-jnp.arange(n_freq, dtype=F32) / n_freq)
    ang = jnp.concatenate([row[:, None] * inv, col[:, None] * inv], axis=-1)
    cos = jnp.cos(ang)
    sin = jnp.sin(ang)
    cos64 = jnp.concatenate([cos, cos], axis=-1)
    sin64 = jnp.concatenate([-sin, sin], axis=-1)
    cos_t = jnp.concatenate([jnp.ones((ctx_len, HEAD_DIM), F32), cos64], axis=0)
    sin_t = jnp.concatenate([jnp.zeros((ctx_len, HEAD_DIM), F32), sin64], axis=0)
    return jnp.tile(cos_t, (1, N_KV_HEADS)), jnp.tile(sin_t, (1, N_KV_HEADS))


def _block_mean_matrix(n, group):
    idx = np.arange(n) // group
    return jnp.asarray((idx[:, None] == idx[None, :]).astype(np.float32) / group, dtype=BF16)


def kernel(x, c, ctx, c_ctx, ada_w, ada_b, norm1_g, norm2_g, ffn_w_in, ffn_w_out, mix_w_in, mix_w_out,
           q_norm_g, k_norm_g, sgu_norm_g, sgu_w, sgu_b, ssm_w_in, ssm_conv_w, ssm_conv_b, ssm_dt_bias,
           ssm_a_log, ssm_d, ssm_norm_g, ssm_w_out):
    batch, seq, _ = x.shape
    ctx_len = ctx.shape[1]
    depth = ada_w.shape[0]
    assert ctx_len % TOKEN_TILE == 0 and seq % TOKEN_TILE == 0 and batch < SUBLANES
    n_ctx_tiles = ctx_len // TOKEN_TILE
    n_ctx_chunks = ctx_len // SSM_CHUNK

    h = jnp.concatenate([ctx, x], axis=1)
    cond = jnp.zeros((SUBLANES, D_MODEL), F32).at[:batch].set(c).at[batch].set(c_ctx)
    mods = _adaln_all(cond, ada_w, ada_b)[:, :, None, :]

    cos_t, sin_t = _rope_tables(ctx_len, seq)
    gq = _block_mean_matrix(ATTN_WIDTH, HEAD_DIM)
    gk = _block_mean_matrix(KV_WIDTH, HEAD_DIM)
    perm = np.arange(ATTN_WIDTH).reshape(N_KV_HEADS, Q_PER_KV, HEAD_DIM).transpose(1, 0, 2).reshape(-1)
    ex_np = np.zeros((2, 2 * LANES, SSM_INNER), np.float32)
    for d in range(2):
        for hd in range(SSM_HEADS):
            for half in range(2):
                ex_np[d, half * LANES + d * SSM_HEADS + hd, hd * SSM_HEAD_DIM:(hd + 1) * SSM_HEAD_DIM] = 1.0
    ex = jnp.asarray(ex_np, dtype=BF16)

    ffn_w_in_b = ffn_w_in.astype(BF16)
    ffn_w_out_b = ffn_w_out.astype(BF16)
    mix_w_in_b = mix_w_in.astype(BF16)
    ssm_w_in_b = ssm_w_in.astype(BF16)
    ssm_w_out_b = ssm_w_out.astype(BF16)
    norm2_rows = norm2_g[:, None, :]
    for i in range(depth):
        j = i // 2
        m = mods[i]
        g1 = norm1_g[i][None, :]
        last = i == depth - 1
        ffn_consts = [(norm2_rows, i), (ffn_w_in_b, i), (ffn_w_out_b, i)]
        if i % 2 == 0:
            qT, k, vT, gu, vn = _mix_in(
                h, m, g1, mix_w_in_b, j, jnp.tile(q_norm_g[j], N_Q_HEADS)[None, :],
                jnp.tile(k_norm_g[j], N_KV_HEADS)[None, :], cos_t, sin_t, gq, gk,
                sgu_norm_g[j][None, :], n_ctx_tiles)
            score_bound = (1.02 * HEAD_DIM ** 0.5 * LOG2E) * jnp.max(jnp.abs(q_norm_g[j])) * jnp.max(jnp.abs(k_norm_g[j]))
            safe = (score_bound <= SAFE_LOG2_BOUND).astype(jnp.int32).reshape(1)
            att = _attention(safe, qT, k, vT, n_ctx_tiles, ctx_len)
            w_out = mix_w_out[j]
            w_out = jnp.concatenate([w_out[:ATTN_WIDTH][perm], w_out[ATTN_WIDTH:]], axis=0).astype(BF16)
            sb = jnp.repeat(sgu_b[j].T, GMLP_WIDTH // GMLP_GROUPS, axis=1)
            h = _layer_tail(_mix_project, "mix_tail", [att, gu, vn], h, m,
                            [sgu_w[j].astype(BF16), sb, w_out], ffn_consts, n_ctx_tiles, last)
        else:
            n_dt = 2 * SSM_HEADS
            wdt = jnp.zeros((D_MODEL, LANES), BF16).at[:, :n_dt].set(ssm_w_in_b[j, :, SSM_INNER + SSM_CONV_DIM:])
            dtb = jnp.zeros((1, LANES), F32).at[0, :n_dt].set(ssm_dt_bias[j].reshape(-1))
            cw = jnp.zeros((SUBLANES, SSM_CONV_DIM), F32).at[:ssm_conv_w.shape[1]].set(ssm_conv_w[j])
            z, xs, bm, cm, dt = _ssm_in(h, m, g1, ssm_w_in_b, j, wdt, cw, ssm_conv_b[j][None, :], dtb, n_ctx_tiles)
            a = -jnp.exp(ssm_a_log[j].astype(F32))
            a_rows = jnp.zeros((2, 1, LANES), F32)
            for d in range(2):
                a_rows = a_rows.at[d, 0, d * SSM_HEADS:(d + 1) * SSM_HEADS].set(a[d])
            ys = _ssd(xs, bm, cm, dt, a_rows, ex, n_ctx_chunks)
            dsk = jnp.repeat(ssm_d[j], SSM_HEAD_DIM)[None, :]
            h = _layer_tail(_ssm_project, "ssm_tail", [ys[0], ys[1], xs, z], h, m,
                            [dsk, ssm_norm_g[j][None, :], (ssm_w_out_b, j)], ffn_consts,
                            n_ctx_tiles, last)
    return h
```

```python
import functools
import math

import jax
import jax.numpy as jnp
import numpy as np
from jax import lax
from jax.experimental import pallas as pl
from jax.experimental.pallas import tpu as pltpu

F32 = jnp.float32
BF16 = jnp.bfloat16

D_MODEL = 1024
GRID_W = 64
HEAD_DIM = 64
ATTN_WIDTH = D_MODEL // 2
N_Q_HEADS = ATTN_WIDTH // HEAD_DIM
N_KV_HEADS = 2
Q_PER_KV = N_Q_HEADS // N_KV_HEADS
KV_WIDTH = N_KV_HEADS * HEAD_DIM
ROPE_THETA = 10000.0
GMLP_WIDTH = D_MODEL // 2
GMLP_CHUNK = 128
GMLP_GROUPS = 4
MIX_IN = ATTN_WIDTH + 2 * KV_WIDTH + 2 * GMLP_WIDTH
SSM_INNER = 2 * D_MODEL
SSM_HEAD_DIM = 64
SSM_HEADS = SSM_INNER // SSM_HEAD_DIM
SSM_GROUPS = 4
SSM_STATE = 128
SSM_CHUNK = 128
SSM_BC_WIDTH = SSM_GROUPS * SSM_STATE
SSM_CONV_DIM = SSM_INNER + 2 * SSM_BC_WIDTH
FFN_HIDDEN = -(-8 * D_MODEL // (3 * 256)) * 256
NORM_EPS = 1e-6

LANES = 128
SUBLANES = 8
TOKEN_TILE = 256
VMEM_LIMIT = 56 * 1024 * 1024
LOG2E = 1.4426950408889634
BF16_ROWS = 16
V_ROWS = HEAD_DIM + BF16_ROWS
SAFE_LOG2_BOUND = 60.0


def _cparams(n_axes):
    return pltpu.CompilerParams(dimension_semantics=("arbitrary",) * n_axes,
                                vmem_limit_bytes=VMEM_LIMIT)


def _silu(x):
    return x / (1.0 + jnp.exp(-x))


def _gelu_tanh(x):
    c = math.sqrt(2.0 / math.pi)
    return x * (0.5 * (1.0 + jnp.tanh(c * (x + 0.044715 * (x * x * x)))))


def _rms_mod(x, g, shift, scale):
    ms = jnp.mean(x * x, axis=-1, keepdims=True)
    return (x * lax.rsqrt(ms + NORM_EPS)) * (g * (1.0 + scale)) + shift


def _split2(x):
    hi = x.astype(BF16)
    lo = (x - hi.astype(F32)).astype(BF16)
    return hi, lo


def _group_mean(x2, gmat):
    hi, lo = _split2(x2)
    return (jnp.dot(hi, gmat, preferred_element_type=F32)
            + jnp.dot(lo, gmat, preferred_element_type=F32))


def _rope(x, cos, sin_signed):
    n = x.shape[-1]
    half = HEAD_DIM // 2
    fwd = pltpu.roll(x, n - half, axis=1)
    bwd = pltpu.roll(x, half, axis=1)
    lane = lax.broadcasted_iota(jnp.int32, x.shape, 1)
    partner = jnp.where((lane % HEAD_DIM) < half, fwd, bwd)
    return x * cos + partner * sin_signed


def _adaln_kernel(c_ref, w_ref, b_ref, o_ref):
    s = _silu(c_ref[...]).astype(BF16)
    o_ref[...] = jnp.dot(s, w_ref[...].astype(BF16), preferred_element_type=F32) + b_ref[...]


def _adaln_all(cond, ada_w, ada_b):
    depth = ada_w.shape[0]
    tn = 1536
    return pl.pallas_call(
        _adaln_kernel,
        grid=(depth, 6 * D_MODEL // tn),
        in_specs=[pl.BlockSpec((SUBLANES, D_MODEL), lambda l, j: (0, 0)),
                  pl.BlockSpec((None, D_MODEL, tn), lambda l, j: (l, 0, j)),
                  pl.BlockSpec((None, 1, tn), lambda l, j: (l, 0, j))],
        out_specs=pl.BlockSpec((None, SUBLANES, tn), lambda l, j: (l, 0, j)),
        out_shape=jax.ShapeDtypeStruct((depth, SUBLANES, 6 * D_MODEL), F32),
        compiler_params=_cparams(2),
        name="adaln",
    )(cond, ada_w, ada_b.reshape(depth, 1, 6 * D_MODEL))


def _mod_spec(n_ctx_tiles, batch):
    return pl.BlockSpec((None, 1, 6 * D_MODEL),
                        lambda b, i: (jnp.where(i < n_ctx_tiles, batch, b), 0, 0))


def _const_spec(shape):
    nd = len(shape)
    return pl.BlockSpec(shape, lambda b, i: (0,) * nd)


def _layer_spec(stacked, layer, **kwargs):
    idx = (layer,) + (0,) * (stacked.ndim - 1)
    return pl.BlockSpec((None,) + stacked.shape[1:], lambda b, i: idx, **kwargs)


def _mix_in_kernel(x_ref, mod_ref, g1_ref, w_ref, qg_ref, kg_ref, cos_ref, sin_ref, gq_ref, gk_ref,
                   sg_ref, qT_ref, k_ref, vT_ref, gu_ref, vn_ref):
    mod = mod_ref[...]
    h = _rms_mod(x_ref[...], g1_ref[...], mod[:, 0:D_MODEL], mod[:, D_MODEL:2 * D_MODEL]).astype(BF16)
    p = jnp.dot(h, w_ref[...], preferred_element_type=F32)
    q = p[:, 0:ATTN_WIDTH]
    k = p[:, ATTN_WIDTH:ATTN_WIDTH + KV_WIDTH]
    v = p[:, ATTN_WIDTH + KV_WIDTH:ATTN_WIDTH + 2 * KV_WIDTH]
    gm = p[:, ATTN_WIDTH + 2 * KV_WIDTH:]
    cos = cos_ref[...]
    sin = sin_ref[...]
    q = q * lax.rsqrt(_group_mean(q * q, gq_ref[...]) + NORM_EPS) * qg_ref[...]
    k = k * lax.rsqrt(_group_mean(k * k, gk_ref[...]) + NORM_EPS) * kg_ref[...]
    reps = ATTN_WIDTH // KV_WIDTH
    q = _rope(q, jnp.concatenate([cos] * reps, axis=1), jnp.concatenate([sin] * reps, axis=1))
    k = _rope(k, cos, sin)
    q = q * (HEAD_DIM ** -0.5 * LOG2E)
    qT_ref[...] = q.T.astype(BF16)
    k_ref[...] = k.astype(BF16)
    vT = v.T
    ones = jnp.ones((BF16_ROWS, vT.shape[1]), F32)
    vT_ref[...] = jnp.concatenate(
        [blk for kvh in range(N_KV_HEADS) for blk in (vT[kvh * HEAD_DIM:(kvh + 1) * HEAD_DIM, :], ones)],
        axis=0).astype(BF16)
    ge = _gelu_tanh(gm)
    gu_ref[...] = ge[:, 0:GMLP_WIDTH].astype(BF16)
    gv = ge[:, GMLP_WIDTH:]
    gd = GMLP_WIDTH // GMLP_GROUPS
    parts = []
    for g in range(GMLP_GROUPS):
        blk = gv[:, g * gd:(g + 1) * gd]
        ms = jnp.mean(blk * blk, axis=-1, keepdims=True)
        parts.append(blk * lax.rsqrt(ms + NORM_EPS))
    vn_ref[...] = (jnp.concatenate(parts, axis=1) * sg_ref[...]).astype(BF16)


def _mix_in(h, mods, g1, w_in_all, layer, qg, kg, cos, sin, gq, gk, sgg, n_ctx_tiles):
    batch, t, _ = h.shape
    tm = TOKEN_TILE
    tok = lambda n: pl.BlockSpec((None, tm, n), lambda b, i: (b, i, 0))
    tr = lambda n: pl.BlockSpec((None, n, tm), lambda b, i: (b, 0, i))
    return pl.pallas_call(
        _mix_in_kernel,
        grid=(batch, t // tm),
        in_specs=[tok(D_MODEL), _mod_spec(n_ctx_tiles, batch), _const_spec((1, D_MODEL)),
                  _layer_spec(w_in_all, layer), _const_spec((1, ATTN_WIDTH)), _const_spec((1, KV_WIDTH)),
                  pl.BlockSpec((tm, KV_WIDTH), lambda b, i: (i, 0)),
                  pl.BlockSpec((tm, KV_WIDTH), lambda b, i: (i, 0)),
                  _const_spec((ATTN_WIDTH, ATTN_WIDTH)), _const_spec((KV_WIDTH, KV_WIDTH)),
                  _const_spec((1, GMLP_WIDTH))],
        out_specs=[tr(ATTN_WIDTH), tok(KV_WIDTH), tr(N_KV_HEADS * V_ROWS), tok(GMLP_WIDTH), tok(GMLP_WIDTH)],
        out_shape=[jax.ShapeDtypeStruct((batch, ATTN_WIDTH, t), BF16),
                   jax.ShapeDtypeStruct((batch, t, KV_WIDTH), BF16),
                   jax.ShapeDtypeStruct((batch, N_KV_HEADS * V_ROWS, t), BF16),
                   jax.ShapeDtypeStruct((batch, t, GMLP_WIDTH), BF16),
                   jax.ShapeDtypeStruct((batch, t, GMLP_WIDTH), BF16)],
        compiler_params=_cparams(2),
        name="mix_in",
    )(h, mods, g1, w_in_all, qg, kg, cos, sin, gq, gk, sgg)


def _attn_kernel(safe_ref, qT_ref, k_ref, vT_ref, o_ref, qpad_ref, acc_ref, m_ref, *, tq, tk, tk_main,
                 n_ctx_tiles, ctx_len, t):
    i = pl.program_id(1)
    for kvh in range(N_KV_HEADS):
        qcat = jnp.concatenate(
            [qT_ref[(kvh * Q_PER_KV + g) * HEAD_DIM:(kvh * Q_PER_KV + g + 1) * HEAD_DIM, :]
             for g in range(Q_PER_KV)], axis=1)
        z = jnp.zeros_like(qcat)
        qpad_ref[kvh] = jnp.concatenate([qcat, z] if kvh == 0 else [z, qcat], axis=0)
    acc_ref[...] = jnp.zeros(acc_ref.shape, F32)
    is_latent = i >= n_ctx_tiles
    safe = safe_ref[0] != 0

    def plain_step(off, width):
        kblk = k_ref[pl.ds(off, width), :]
        for kvh in range(N_KV_HEADS):
            s = jnp.dot(kblk, qpad_ref[kvh], preferred_element_type=F32)
            p = jnp.exp2(s).astype(BF16)
            vblk = vT_ref[kvh * V_ROWS:(kvh + 1) * V_ROWS, pl.ds(off, width)]
            acc_ref[kvh] += jnp.dot(vblk, p, preferred_element_type=F32)

    @pl.when(safe)
    def _():
        plain_step(0, ctx_len)

        @pl.when(is_latent)
        def _():
            def body(j, carry):
                plain_step(pl.multiple_of(ctx_len + j * tk_main, LANES), tk_main)
                return carry
            lax.fori_loop(0, (t - ctx_len) // tk_main, body, 0, unroll=4)

    @pl.when(jnp.logical_not(safe))
    def _():
        m_ref[...] = jnp.full(m_ref.shape, -jnp.inf, F32)

        def body(j, carry):
            off = pl.multiple_of(j * tk, tk)
            kblk = k_ref[pl.ds(off, tk), :]
            for kvh in range(N_KV_HEADS):
                s = jnp.dot(kblk, qpad_ref[kvh], preferred_element_type=F32)
                m_old = m_ref[kvh]
                m_new = jnp.maximum(m_old, jnp.max(s, axis=0, keepdims=True))
                alpha = jnp.exp2(m_old - m_new)
                p = jnp.exp2(s - m_new).astype(BF16)
                vblk = vT_ref[kvh * V_ROWS:(kvh + 1) * V_ROWS, pl.ds(off, tk)]
                acc_ref[kvh] = acc_ref[kvh] * alpha + jnp.dot(vblk, p, preferred_element_type=F32)
                m_ref[kvh] = m_new
            return carry

        lax.fori_loop(0, jnp.where(is_latent, t // tk, ctx_len // tk), body, 0)

    for g in range(Q_PER_KV):
        og = jnp.concatenate(
            [acc_ref[kvh, 0:HEAD_DIM, g * tq:(g + 1) * tq]
             / acc_ref[kvh, HEAD_DIM:HEAD_DIM + 1, g * tq:(g + 1) * tq]
             for kvh in range(N_KV_HEADS)], axis=0)
        o_ref[:, g * KV_WIDTH:(g + 1) * KV_WIDTH] = og.T.astype(BF16)


def _attention(safe, qT, k, vT, n_ctx_tiles, ctx_len):
    batch, _, t = qT.shape
    tq = TOKEN_TILE
    tk = 256
    tk_main = 1024
    assert ctx_len % tk == 0 and t % tk == 0 and (t - ctx_len) % tk_main == 0
    kern = functools.partial(_attn_kernel, tq=tq, tk=tk, tk_main=tk_main, n_ctx_tiles=n_ctx_tiles,
                             ctx_len=ctx_len, t=t)
    return pl.pallas_call(
        kern,
        grid=(batch, t // tq),
        in_specs=[pl.BlockSpec(memory_space=pltpu.SMEM),
                  pl.BlockSpec((None, ATTN_WIDTH, tq), lambda b, i: (b, 0, i)),
                  pl.BlockSpec((None, t, KV_WIDTH), lambda b, i: (b, 0, 0)),
                  pl.BlockSpec((None, N_KV_HEADS * V_ROWS, t), lambda b, i: (b, 0, 0))],
        out_specs=pl.BlockSpec((None, tq, ATTN_WIDTH), lambda b, i: (b, i, 0)),
        out_shape=jax.ShapeDtypeStruct((batch, t, ATTN_WIDTH), BF16),
        scratch_shapes=[pltpu.VMEM((N_KV_HEADS, KV_WIDTH, Q_PER_KV * tq), BF16),
                        pltpu.VMEM((N_KV_HEADS, V_ROWS, Q_PER_KV * tq), F32),
                        pltpu.VMEM((N_KV_HEADS, 1, Q_PER_KV * tq), F32)],
        compiler_params=_cparams(2),
        name="attention",
    )(safe, qT, k, vT)


FFN_SPLIT = 1


def _mix_project(att_ref, gu_ref, vn_ref, sw_ref, sb_ref, wo_ref):
    tm = att_ref.shape[0]
    gd = GMLP_WIDTH // GMLP_GROUPS
    chunks = []
    for c in range(tm // GMLP_CHUNK):
        rows = slice(c * GMLP_CHUNK, (c + 1) * GMLP_CHUNK)
        s = jnp.concatenate(
            [jnp.dot(sw_ref[g], vn_ref[rows, g * gd:(g + 1) * gd], preferred_element_type=F32)
             for g in range(GMLP_GROUPS)], axis=1) + sb_ref[...]
        chunks.append(gu_ref[rows, :].astype(F32) * s)
    sg = jnp.concatenate(chunks, axis=0).astype(BF16)
    return (jnp.dot(att_ref[...], wo_ref[0:ATTN_WIDTH, :], preferred_element_type=F32)
            + jnp.dot(sg, wo_ref[ATTN_WIDTH:, :], preferred_element_type=F32))


def _ssm_project(yf_ref, yb_ref, xs_ref, z_ref, dsk_ref, ng_ref, wo_ref):
    gw = SSM_INNER // SSM_GROUPS
    out = None
    for g in range(SSM_GROUPS):
        cols = slice(g * gw, (g + 1) * gw)
        y = (yf_ref[:, cols].astype(F32) + yb_ref[:, cols].astype(F32)
             + dsk_ref[:, cols] * xs_ref[:, cols].astype(F32))
        y = y * _silu(z_ref[:, cols].astype(F32))
        ms = jnp.mean(y * y, axis=-1, keepdims=True)
        yn = (y * lax.rsqrt(ms + NORM_EPS) * ng_ref[:, cols]).astype(BF16)
        part = jnp.dot(yn, wo_ref[cols, :], preferred_element_type=F32)
        out = part if out is None else out + part
    return out


def _tail_kernel(*refs, project, n_tok, n_proj):
    tok_refs = refs[:n_tok]
    x_ref, mod_ref = refs[n_tok:n_tok + 2]
    proj_refs = refs[n_tok + 2:n_tok + 2 + n_proj]
    g2_ref, wi_ref, wf_ref, o_ref = refs[n_tok + 2 + n_proj:]
    mod = mod_ref[...]
    x = x_ref[...] + mod[:, 2 * D_MODEL:3 * D_MODEL] * project(*tok_refs, *proj_refs)
    h = _rms_mod(x, g2_ref[...], mod[:, 3 * D_MODEL:4 * D_MODEL], mod[:, 4 * D_MODEL:5 * D_MODEL]).astype(BF16)
    hc = FFN_HIDDEN // FFN_SPLIT
    acc = jnp.zeros(x.shape, F32)
    for c in range(FFN_SPLIT):
        g = jnp.dot(h, wi_ref[:, c * hc:(c + 1) * hc], preferred_element_type=F32)
        u = jnp.dot(h, wi_ref[:, FFN_HIDDEN + c * hc:FFN_HIDDEN + (c + 1) * hc], preferred_element_type=F32)
        a = (_silu(g) * u).astype(BF16)
        acc = acc + jnp.dot(a, wf_ref[c * hc:(c + 1) * hc, :], preferred_element_type=F32)
    o_ref[...] = x + mod[:, 5 * D_MODEL:6 * D_MODEL] * acc


def _layer_tail(project, name, tok_inputs, h, mods, proj_consts, ffn_consts, n_ctx_tiles, latent_only):
    batch, t, _ = h.shape
    tm = TOKEN_TILE
    skip = n_ctx_tiles if latent_only else 0
    tok = lambda n: pl.BlockSpec((None, tm, n), lambda b, i: (b, i + skip, 0))
    mod_spec = pl.BlockSpec((None, 1, 6 * D_MODEL),
                            lambda b, i: (jnp.where(i + skip < n_ctx_tiles, batch, b), 0, 0))
    def resident(c):
        if isinstance(c, tuple):
            return _layer_spec(*c, pipeline_mode=pl.Buffered(1))
        return pl.BlockSpec(c.shape, lambda b, i, nd=c.ndim: (0,) * nd, pipeline_mode=pl.Buffered(1))

    consts = list(proj_consts) + list(ffn_consts)
    arrays = [c[0] if isinstance(c, tuple) else c for c in consts]
    kern = functools.partial(_tail_kernel, project=project, n_tok=len(tok_inputs), n_proj=len(proj_consts))
    return pl.pallas_call(
        kern,
        grid=(batch, t // tm - skip),
        in_specs=([tok(a.shape[-1]) for a in tok_inputs] + [tok(D_MODEL), mod_spec]
                  + [resident(a) for a in consts]),
        out_specs=pl.BlockSpec((None, tm, D_MODEL), lambda b, i: (b, i, 0)),
        out_shape=jax.ShapeDtypeStruct((batch, t - skip * tm, D_MODEL), F32),
        compiler_params=_cparams(2),
        name=name,
    )(*tok_inputs, h, mods, *arrays)


def _ssm_in_kernel(xp_ref, x_ref, xn_ref, mod_ref, g1_ref, w_ref, wdt_ref, cw_ref, cb_ref, dtb_ref,
                   z_ref, xs_ref, bm_ref, cm_ref, dt_ref, *, n_tiles, n_ctx_tiles, col_block):
    i = pl.program_id(1)
    tm = x_ref.shape[0]
    halo = SUBLANES
    prev_ok = jnp.logical_and(i != 0, i != n_ctx_tiles)
    next_ok = jnp.logical_and(i != n_ctx_tiles - 1, i != n_tiles - 1)
    xcat = jnp.concatenate([x_ref[...], xn_ref[...], xp_ref[...]], axis=0)
    mod = mod_ref[...]
    h = _rms_mod(xcat, g1_ref[...], mod[:, 0:D_MODEL], mod[:, D_MODEL:2 * D_MODEL]).astype(BF16)
    hm = h[0:tm, :]
    z_ref[...] = jnp.dot(hm, w_ref[:, 0:SSM_INNER], preferred_element_type=F32).astype(BF16)
    dtr = jnp.dot(hm, wdt_ref[...], preferred_element_type=F32) + dtb_ref[...]
    dt_ref[...] = jnp.maximum(dtr, 0.0) + jnp.log1p(jnp.exp(-jnp.abs(dtr)))
    prev_f = prev_ok.astype(F32)
    next_f = next_ok.astype(F32)
    for blk in range(SSM_CONV_DIM // col_block):
        cols = slice(blk * col_block, (blk + 1) * col_block)
        p = jnp.dot(h, w_ref[:, SSM_INNER + blk * col_block:SSM_INNER + (blk + 1) * col_block],
                    preferred_element_type=F32)
        p = jnp.concatenate([p[0:tm, :], p[tm:tm + halo, :] * next_f, p[tm + halo:, :] * prev_f], axis=0)
        pm1 = pltpu.roll(p, 1, axis=0)[0:tm, :]
        pp1 = pltpu.roll(p, tm + 2 * halo - 1, axis=0)[0:tm, :]
        y = (cw_ref[0:1, cols] * pm1 + cw_ref[1:2, cols] * p[0:tm, :]
             + cw_ref[2:3, cols] * pp1 + cb_ref[:, cols])
        y = _silu(y).astype(BF16)
        c0 = blk * col_block
        if c0 < SSM_INNER:
            xs_ref[:, c0:c0 + col_block] = y
        elif c0 < SSM_INNER + SSM_BC_WIDTH:
            bm_ref[:, c0 - SSM_INNER:c0 - SSM_INNER + col_block] = y
        else:
            o0 = c0 - SSM_INNER - SSM_BC_WIDTH
            cm_ref[:, o0:o0 + col_block] = y


def _ssm_in(h, mods, g1, w_in_all, layer, wdt, cw, cb, dtb, n_ctx_tiles):
    batch, t, _ = h.shape
    tm = TOKEN_TILE
    n_tiles = t // tm
    per = tm // SUBLANES
    tok = lambda n: pl.BlockSpec((None, tm, n), lambda b, i: (b, i, 0))
    col_block = 512
    kern = functools.partial(_ssm_in_kernel, n_tiles=n_tiles, n_ctx_tiles=n_ctx_tiles, col_block=col_block)
    return pl.pallas_call(
        kern,
        grid=(batch, n_tiles),
        in_specs=[pl.BlockSpec((None, SUBLANES, D_MODEL), lambda b, i: (b, jnp.maximum(i * per - 1, 0), 0)),
                  tok(D_MODEL),
                  pl.BlockSpec((None, SUBLANES, D_MODEL),
                               lambda b, i: (b, jnp.minimum((i + 1) * per, t // SUBLANES - 1), 0)),
                  _mod_spec(n_ctx_tiles, batch), _const_spec((1, D_MODEL)),
                  _layer_spec(w_in_all, layer),
                  _const_spec((D_MODEL, LANES)), _const_spec((SUBLANES, SSM_CONV_DIM)),
                  _const_spec((1, SSM_CONV_DIM)), _const_spec((1, LANES))],
        out_specs=[tok(SSM_INNER), tok(SSM_INNER), tok(SSM_BC_WIDTH), tok(SSM_BC_WIDTH), tok(LANES)],
        out_shape=[jax.ShapeDtypeStruct((batch, t, SSM_INNER), BF16),
                   jax.ShapeDtypeStruct((batch, t, SSM_INNER), BF16),
                   jax.ShapeDtypeStruct((batch, t, SSM_BC_WIDTH), BF16),
                   jax.ShapeDtypeStruct((batch, t, SSM_BC_WIDTH), BF16),
                   jax.ShapeDtypeStruct((batch, t, LANES), F32)],
        compiler_params=_cparams(2),
        name="ssm_in",
    )(h, h, h, mods, g1, w_in_all, wdt, cw, cb, dtb)


def _ssd_chunk(xs_ref, bm_ref, cm_ref, dt_ref, a_ref, ex_ref, y_ref, st_ref, *, reverse, lane_base):
    L = SSM_CHUNK

    @pl.when(pl.program_id(1) == 0)
    def _():
        st_ref[...] = jnp.zeros(st_ref.shape, F32)

    dt = dt_ref[...]
    da = dt * a_ref[...]
    r = lax.broadcasted_iota(jnp.int32, (L, L), 0)
    c = lax.broadcasted_iota(jnp.int32, (L, L), 1)
    before = (c >= r) if reverse else (c <= r)
    tri = jnp.where(before, 1.0, 0.0).astype(BF16)
    hi = da.astype(BF16)
    r1 = da - hi.astype(F32)
    mid = r1.astype(BF16)
    lo = (r1 - mid.astype(F32)).astype(BF16)
    p3 = jnp.dot(tri, jnp.concatenate([hi, mid, lo], axis=1), preferred_element_type=F32)
    P = p3[:, 0:LANES] + p3[:, LANES:2 * LANES] + p3[:, 2 * LANES:3 * LANES]
    tot = jnp.sum(da, axis=0, keepdims=True)
    QT = (P - jnp.log(dt)).T

    def expand(v):
        return jnp.dot(jnp.concatenate(_split2(v), axis=1), ex_ref[...], preferred_element_type=F32)

    w_state_all = expand(dt * jnp.exp(tot - P))
    e_p_all = expand(jnp.exp(P))
    end = 0 if reverse else L - 1
    lane = lax.broadcasted_iota(jnp.int32, (1, LANES), 1)
    first_head = jnp.where(lane < SSM_HEAD_DIM, 1.0, 0.0).astype(BF16)
    second_head = jnp.where(lane >= SSM_HEAD_DIM, 1.0, 0.0).astype(BF16)
    heads_per_group = SSM_HEADS // SSM_GROUPS
    gw = heads_per_group * SSM_HEAD_DIM
    for g in range(SSM_GROUPS):
        e_p_x = e_p_all[:, g * gw:(g + 1) * gw]
        bg = bm_ref[:, g * SSM_STATE:(g + 1) * SSM_STATE]
        cg = cm_ref[:, g * SSM_STATE:(g + 1) * SSM_STATE]
        cb = lax.dot_general(cg, bg, (((1,), (1,)), ((), ())), preferred_element_type=F32)
        st = st_ref[g]
        y_off = jnp.dot(cg, st.astype(BF16), preferred_element_type=F32) * e_p_x
        ys = []
        for pair in range(heads_per_group // 2):
            ms = []
            for k in range(2):
                hl = lane_base + g * heads_per_group + 2 * pair + k
                seg = P[:, hl:hl + 1] - QT[hl:hl + 1, :]
                ms.append((cb * jnp.exp(jnp.where(before, seg, -jnp.inf))).astype(BF16))
            xp = xs_ref[:, g * gw + pair * LANES:g * gw + (pair + 1) * LANES]
            rhs = jnp.concatenate([xp * first_head, xp * second_head], axis=0)
            ys.append(jnp.dot(jnp.concatenate(ms, axis=1), rhs, preferred_element_type=F32))
        y_ref[:, g * gw:(g + 1) * gw] = (jnp.concatenate(ys, axis=1) + y_off).astype(BF16)
        xdd = xs_ref[:, g * gw:(g + 1) * gw] * w_state_all[:, g * gw:(g + 1) * gw].astype(BF16)
        bgT = bg.astype(F32).T.astype(BF16)
        st_ref[g] = st * e_p_x[end:end + 1, :] + jnp.dot(bgT, xdd, preferred_element_type=F32)


def _ssd_pair_kernel(xf_ref, bf_ref, cf_ref, dtf_ref, xb_ref, bb_ref, cb_ref, dtb_ref, a_ref, ex_ref,
                     yf_ref, yb_ref, stf_ref, stb_ref):
    _ssd_chunk(xf_ref, bf_ref, cf_ref, dtf_ref, a_ref.at[0], ex_ref.at[0], yf_ref, stf_ref,
               reverse=False, lane_base=0)
    _ssd_chunk(xb_ref, bb_ref, cb_ref, dtb_ref, a_ref.at[1], ex_ref.at[1], yb_ref, stb_ref,
               reverse=True, lane_base=SSM_HEADS)


def _ssd(xs, bm, cm, dt, a_rows, ex, n_ctx_chunks):
    batch, t, _ = xs.shape
    L = SSM_CHUNK
    nc = t // L
    fwd = lambda i: i
    bwd = lambda i: jnp.where(i < n_ctx_chunks, n_ctx_chunks - 1 - i, nc - 1 + n_ctx_chunks - i)
    tok = lambda n, cmap: pl.BlockSpec((None, L, n), lambda b, i: (b, cmap(i), 0))
    chunk_specs = lambda cmap: [tok(SSM_INNER, cmap), tok(SSM_BC_WIDTH, cmap), tok(SSM_BC_WIDTH, cmap),
                                tok(LANES, cmap)]
    state = pltpu.VMEM((SSM_GROUPS, SSM_STATE, SSM_INNER // SSM_GROUPS), F32)
    y_shape = jax.ShapeDtypeStruct((batch, t, SSM_INNER), BF16)
    return pl.pallas_call(
        _ssd_pair_kernel,
        grid=(batch, nc),
        in_specs=(chunk_specs(fwd) + chunk_specs(bwd)
                  + [_const_spec((2, 1, LANES)), _const_spec((2, 2 * LANES, SSM_INNER))]),
        out_specs=[tok(SSM_INNER, fwd), tok(SSM_INNER, bwd)],
        out_shape=[y_shape, y_shape],
        scratch_shapes=[state, state],
        compiler_params=_cparams(2),
        name="ssd",
    )(xs, bm, cm, dt, xs, bm, cm, dt, a_rows, ex)


def _rope_tables(ctx_len, seq):
    tpos = jnp.arange(seq)
    row = (tpos // GRID_W).astype(F32)
    col = (tpos % GRID_W).astype(F32)
    n_freq = HEAD_DIM // 4
    inv = ROPE_THETA ** (-jnp.arange(n_freq, dtype=F32) / n_freq)
    ang = jnp.concatenate([row[:, None] * inv, col[:, None] * inv], axis=-1)
    cos = jnp.cos(ang)
    sin = jnp.sin(ang)
    cos64 = jnp.concatenate([cos, cos], axis=-1)
    sin64 = jnp.concatenate([-sin, sin], axis=-1)
    cos_t = jnp.concatenate([jnp.ones((ctx_len, HEAD_DIM), F32), cos64], axis=0)
    sin_t = jnp.concatenate([jnp.zeros((ctx_len, HEAD_DIM), F32), sin64], axis=0)
    return jnp.tile(cos_t, (1, N_KV_HEADS)), jnp.tile(sin_t, (1, N_KV_HEADS))


def _block_mean_matrix(n, group):
    idx = np.arange(n) // group
    return jnp.asarray((idx[:, None] == idx[None, :]).astype(np.float32) / group, dtype=BF16)


def kernel(x, c, ctx, c_ctx, ada_w, ada_b, norm1_g, norm2_g, ffn_w_in, ffn_w_out, mix_w_in, mix_w_out,
           q_norm_g, k_norm_g, sgu_norm_g, sgu_w, sgu_b, ssm_w_in, ssm_conv_w, ssm_conv_b, ssm_dt_bias,
           ssm_a_log, ssm_d, ssm_norm_g, ssm_w_out):
    batch, seq, _ = x.shape
    ctx_len = ctx.shape[1]
    depth = ada_w.shape[0]
    assert ctx_len % TOKEN_TILE == 0 and seq % TOKEN_TILE == 0 and batch < SUBLANES
    n_ctx_tiles = ctx_len // TOKEN_TILE
    n_ctx_chunks = ctx_len // SSM_CHUNK

    h = jnp.concatenate([ctx, x], axis=1)
    cond = jnp.zeros((SUBLANES, D_MODEL), F32).at[:batch].set(c).at[batch].set(c_ctx)
    mods = _adaln_all(cond, ada_w, ada_b)[:, :, None, :]

    cos_t, sin_t = _rope_tables(ctx_len, seq)
    gq = _block_mean_matrix(ATTN_WIDTH, HEAD_DIM)
    gk = _block_mean_matrix(KV_WIDTH, HEAD_DIM)
    perm = np.arange(ATTN_WIDTH).reshape(N_KV_HEADS, Q_PER_KV, HEAD_DIM).transpose(1, 0, 2).reshape(-1)
    ex_np = np.zeros((2, 2 * LANES, SSM_INNER), np.float32)
    for d in range(2):
        for hd in range(SSM_HEADS):
            for half in range(2):
                ex_np[d, half * LANES + d * SSM_HEADS + hd, hd * SSM_HEAD_DIM:(hd + 1) * SSM_HEAD_DIM] = 1.0
    ex = jnp.asarray(ex_np, dtype=BF16)

    ffn_w_in_b = ffn_w_in.astype(BF16)
    ffn_w_out_b = ffn_w_out.astype(BF16)
    mix_w_in_b = mix_w_in.astype(BF16)
    ssm_w_in_b = ssm_w_in.astype(BF16)
    ssm_w_out_b = ssm_w_out.astype(BF16)
    norm2_rows = norm2_g[:, None, :]
    for i in range(depth):
        j = i // 2
        m = mods[i]
        g1 = norm1_g[i][None, :]
        last = i == depth - 1
        ffn_consts = [(norm2_rows, i), (ffn_w_in_b, i), (ffn_w_out_b, i)]
        if i % 2 == 0:
            qT, k, vT, gu, vn = _mix_in(
                h, m, g1, mix_w_in_b, j, jnp.tile(q_norm_g[j], N_Q_HEADS)[None, :],
                jnp.tile(k_norm_g[j], N_KV_HEADS)[None, :], cos_t, sin_t, gq, gk,
                sgu_norm_g[j][None, :], n_ctx_tiles)
            score_bound = (1.02 * HEAD_DIM ** 0.5 * LOG2E) * jnp.max(jnp.abs(q_norm_g[j])) * jnp.max(jnp.abs(k_norm_g[j]))
            safe = (score_bound <= SAFE_LOG2_BOUND).astype(jnp.int32).reshape(1)
            att = _attention(safe, qT, k, vT, n_ctx_tiles, ctx_len)
            w_out = mix_w_out[j]
            w_out = jnp.concatenate([w_out[:ATTN_WIDTH][perm], w_out[ATTN_WIDTH:]], axis=0).astype(BF16)
            sb = jnp.repeat(sgu_b[j].T, GMLP_WIDTH // GMLP_GROUPS, axis=1)
            h = _layer_tail(_mix_project, "mix_tail", [att, gu, vn], h, m,
                            [sgu_w[j].astype(BF16), sb, w_out], ffn_consts, n_ctx_tiles, last)
        else:
            n_dt = 2 * SSM_HEADS
            wdt = jnp.zeros((D_MODEL, LANES), BF16).at[:, :n_dt].set(ssm_w_in_b[j, :, SSM_INNER + SSM_CONV_DIM:])
            dtb = jnp.zeros((1, LANES), F32).at[0, :n_dt].set(ssm_dt_bias[j].reshape(-1))
            cw = jnp.zeros((SUBLANES, SSM_CONV_DIM), F32).at[:ssm_conv_w.shape[1]].set(ssm_conv_w[j])
            z, xs, bm, cm, dt = _ssm_in(h, m, g1, ssm_w_in_b, j, wdt, cw, ssm_conv_b[j][None, :], dtb, n_ctx_tiles)
            a = -jnp.exp(ssm_a_log[j].astype(F32))
            a_rows = jnp.zeros((2, 1, LANES), F32)
            for d in range(2):
                a_rows = a_rows.at[d, 0, d * SSM_HEADS:(d + 1) * SSM_HEADS].set(a[d])
            ys = _ssd(xs, bm, cm, dt, a_rows, ex, n_ctx_chunks)
            dsk = jnp.repeat(ssm_d[j], SSM_HEAD_DIM)[None, :]
            h = _layer_tail(_ssm_project, "ssm_tail", [ys[0], ys[1], xs, z], h, m,
                            [dsk, ssm_norm_g[j][None, :], (ssm_w_out_b, j)], ffn_consts,
                            n_ctx_tiles, last)
    return h
```

```python
import functools
import math

import jax
import jax.numpy as jnp
import numpy as np
from jax import lax
from jax.experimental import pallas as pl
from jax.experimental.pallas import tpu as pltpu

F32 = jnp.float32
BF16 = jnp.bfloat16

D_MODEL = 1024
GRID_W = 64
HEAD_DIM = 64
ATTN_WIDTH = D_MODEL // 2
N_Q_HEADS = ATTN_WIDTH // HEAD_DIM
N_KV_HEADS = 2
Q_PER_KV = N_Q_HEADS // N_KV_HEADS
KV_WIDTH = N_KV_HEADS * HEAD_DIM
ROPE_THETA = 10000.0
GMLP_WIDTH = D_MODEL // 2
GMLP_CHUNK = 128
GMLP_GROUPS = 4
MIX_IN = ATTN_WIDTH + 2 * KV_WIDTH + 2 * GMLP_WIDTH
SSM_INNER = 2 * D_MODEL
SSM_HEAD_DIM = 64
SSM_HEADS = SSM_INNER // SSM_HEAD_DIM
SSM_GROUPS = 4
SSM_STATE = 128
SSM_CHUNK = 128
SSM_BC_WIDTH = SSM_GROUPS * SSM_STATE
SSM_CONV_DIM = SSM_INNER + 2 * SSM_BC_WIDTH
FFN_HIDDEN = -(-8 * D_MODEL // (3 * 256)) * 256
NORM_EPS = 1e-6

LANES = 128
SUBLANES = 8
TOKEN_TILE = 256
VMEM_LIMIT = 56 * 1024 * 1024
LOG2E = 1.4426950408889634
BF16_ROWS = 16
V_ROWS = HEAD_DIM + BF16_ROWS
SAFE_LOG2_BOUND = 60.0


def _cparams(n_axes):
    return pltpu.CompilerParams(dimension_semantics=("arbitrary",) * n_axes,
                                vmem_limit_bytes=VMEM_LIMIT)


def _silu(x):
    return x / (1.0 + jnp.exp2(x * (-LOG2E)))


def _gelu_tanh(x):
    c = math.sqrt(2.0 / math.pi)
    return x * (0.5 * (1.0 + jnp.tanh(c * (x + 0.044715 * (x * x * x)))))


def _rms_mod(x, g, shift, scale):
    ms = jnp.mean(x * x, axis=-1, keepdims=True)
    return (x * lax.rsqrt(ms + NORM_EPS)) * (g * (1.0 + scale)) + shift


def _split2(x):
    hi = x.astype(BF16)
    lo = (x - hi.astype(F32)).astype(BF16)
    return hi, lo


def _group_mean(x2, gmat):
    hi, lo = _split2(x2)
    return (jnp.dot(hi, gmat, preferred_element_type=F32)
            + jnp.dot(lo, gmat, preferred_element_type=F32))


def _rope(x, cos, sin_signed):
    n = x.shape[-1]
    half = HEAD_DIM // 2
    fwd = pltpu.roll(x, n - half, axis=1)
    bwd = pltpu.roll(x, half, axis=1)
    lane = lax.broadcasted_iota(jnp.int32, x.shape, 1)
    partner = jnp.where((lane % HEAD_DIM) < half, fwd, bwd)
    return x * cos + partner * sin_signed


def _adaln_kernel(c_ref, w_ref, b_ref, o_ref):
    s = _silu(c_ref[...]).astype(BF16)
    o_ref[...] = jnp.dot(s, w_ref[...].astype(BF16), preferred_element_type=F32) + b_ref[...]


def _adaln_all(cond, ada_w, ada_b):
    depth = ada_w.shape[0]
    tn = 1536
    return pl.pallas_call(
        _adaln_kernel,
        grid=(depth, 6 * D_MODEL // tn),
        in_specs=[pl.BlockSpec((SUBLANES, D_MODEL), lambda l, j: (0, 0)),
                  pl.BlockSpec((None, D_MODEL, tn), lambda l, j: (l, 0, j)),
                  pl.BlockSpec((None, 1, tn), lambda l, j: (l, 0, j))],
        out_specs=pl.BlockSpec((None, SUBLANES, tn), lambda l, j: (l, 0, j)),
        out_shape=jax.ShapeDtypeStruct((depth, SUBLANES, 6 * D_MODEL), F32),
        compiler_params=_cparams(2),
        name="adaln",
    )(cond, ada_w, ada_b.reshape(depth, 1, 6 * D_MODEL))


def _mod_spec(n_ctx_tiles, batch):
    return pl.BlockSpec((None, 1, 6 * D_MODEL),
                        lambda b, i: (jnp.where(i < n_ctx_tiles, batch, b), 0, 0))


def _const_spec(shape):
    nd = len(shape)
    return pl.BlockSpec(shape, lambda b, i: (0,) * nd)


def _layer_spec(stacked, layer, **kwargs):
    idx = (layer,) + (0,) * (stacked.ndim - 1)
    return pl.BlockSpec((None,) + stacked.shape[1:], lambda b, i: idx, **kwargs)


def _mix_in_kernel(x_ref, mod_ref, g1_ref, w_ref, qg_ref, kg_ref, cos_ref, sin_ref, gq_ref, gk_ref,
                   sg_ref, qT_ref, k_ref, vT_ref, gu_ref, vn_ref):
    mod = mod_ref[...]
    h = _rms_mod(x_ref[...], g1_ref[...], mod[:, 0:D_MODEL], mod[:, D_MODEL:2 * D_MODEL]).astype(BF16)
    p = jnp.dot(h, w_ref[...], preferred_element_type=F32)
    q = p[:, 0:ATTN_WIDTH]
    k = p[:, ATTN_WIDTH:ATTN_WIDTH + KV_WIDTH]
    v = p[:, ATTN_WIDTH + KV_WIDTH:ATTN_WIDTH + 2 * KV_WIDTH]
    gm = p[:, ATTN_WIDTH + 2 * KV_WIDTH:]
    cos = cos_ref[...]
    sin = sin_ref[...]
    q = q * lax.rsqrt(_group_mean(q * q, gq_ref[...]) + NORM_EPS) * qg_ref[...]
    k = k * lax.rsqrt(_group_mean(k * k, gk_ref[...]) + NORM_EPS) * kg_ref[...]
    reps = ATTN_WIDTH // KV_WIDTH
    q = _rope(q, jnp.concatenate([cos] * reps, axis=1), jnp.concatenate([sin] * reps, axis=1))
    k = _rope(k, cos, sin)
    q = q * (HEAD_DIM ** -0.5 * LOG2E)
    qT_ref[...] = q.T.astype(BF16)
    k_ref[...] = k.astype(BF16)
    vT = v.T
    ones = jnp.ones((BF16_ROWS, vT.shape[1]), F32)
    vT_ref[...] = jnp.concatenate(
        [blk for kvh in range(N_KV_HEADS) for blk in (vT[kvh * HEAD_DIM:(kvh + 1) * HEAD_DIM, :], ones)],
        axis=0).astype(BF16)
    ge = _gelu_tanh(gm)
    gu_ref[...] = ge[:, 0:GMLP_WIDTH].astype(BF16)
    gv = ge[:, GMLP_WIDTH:]
    gd = GMLP_WIDTH // GMLP_GROUPS
    parts = []
    for g in range(GMLP_GROUPS):
        blk = gv[:, g * gd:(g + 1) * gd]
        ms = jnp.mean(blk * blk, axis=-1, keepdims=True)
        parts.append(blk * lax.rsqrt(ms + NORM_EPS))
    vn_ref[...] = (jnp.concatenate(parts, axis=1) * sg_ref[...]).astype(BF16)


def _mix_in(h, mods, g1, w_in_all, layer, qg, kg, cos, sin, gq, gk, sgg, n_ctx_tiles):
    batch, t, _ = h.shape
    tm = TOKEN_TILE
    tok = lambda n: pl.BlockSpec((None, tm, n), lambda b, i: (b, i, 0))
    tr = lambda n: pl.BlockSpec((None, n, tm), lambda b, i: (b, 0, i))
    return pl.pallas_call(
        _mix_in_kernel,
        grid=(batch, t // tm),
        in_specs=[tok(D_MODEL), _mod_spec(n_ctx_tiles, batch), _const_spec((1, D_MODEL)),
                  _layer_spec(w_in_all, layer), _const_spec((1, ATTN_WIDTH)), _const_spec((1, KV_WIDTH)),
                  pl.BlockSpec((tm, KV_WIDTH), lambda b, i: (i, 0)),
                  pl.BlockSpec((tm, KV_WIDTH), lambda b, i: (i, 0)),
                  _const_spec((ATTN_WIDTH, ATTN_WIDTH)), _const_spec((KV_WIDTH, KV_WIDTH)),
                  _const_spec((1, GMLP_WIDTH))],
        out_specs=[tr(ATTN_WIDTH), tok(KV_WIDTH), tr(N_KV_HEADS * V_ROWS), tok(GMLP_WIDTH), tok(GMLP_WIDTH)],
        out_shape=[jax.ShapeDtypeStruct((batch, ATTN_WIDTH, t), BF16),
                   jax.ShapeDtypeStruct((batch, t, KV_WIDTH), BF16),
                   jax.ShapeDtypeStruct((batch, N_KV_HEADS * V_ROWS, t), BF16),
                   jax.ShapeDtypeStruct((batch, t, GMLP_WIDTH), BF16),
                   jax.ShapeDtypeStruct((batch, t, GMLP_WIDTH), BF16)],
        compiler_params=_cparams(2),
        name="mix_in",
    )(h, mods, g1, w_in_all, qg, kg, cos, sin, gq, gk, sgg)


def _attn_kernel(safe_ref, qT_ref, k_ref, vT_ref, o_ref, qpad_ref, acc_ref, m_ref, *, tq, tk, tk_main,
                 n_ctx_tiles, ctx_len, t):
    i = pl.program_id(1)
    for kvh in range(N_KV_HEADS):
        qcat = jnp.concatenate(
            [qT_ref[(kvh * Q_PER_KV + g) * HEAD_DIM:(kvh * Q_PER_KV + g + 1) * HEAD_DIM, :]
             for g in range(Q_PER_KV)], axis=1)
        z = jnp.zeros_like(qcat)
        qpad_ref[kvh] = jnp.concatenate([qcat, z] if kvh == 0 else [z, qcat], axis=0)
    acc_ref[...] = jnp.zeros(acc_ref.shape, F32)
    is_latent = i >= n_ctx_tiles
    safe = safe_ref[0] != 0

    def plain_step(off, width):
        kblk = k_ref[pl.ds(off, width), :]
        for kvh in range(N_KV_HEADS):
            s = jnp.dot(kblk, qpad_ref[kvh], preferred_element_type=F32)
            p = jnp.exp2(s).astype(BF16)
            vblk = vT_ref[kvh * V_ROWS:(kvh + 1) * V_ROWS, pl.ds(off, width)]
            acc_ref[kvh] += jnp.dot(vblk, p, preferred_element_type=F32)

    @pl.when(safe)
    def _():
        plain_step(0, ctx_len)

        @pl.when(is_latent)
        def _():
            def body(j, carry):
                plain_step(pl.multiple_of(ctx_len + j * tk_main, LANES), tk_main)
                return carry
            lax.fori_loop(0, (t - ctx_len) // tk_main, body, 0, unroll=4)

    @pl.when(jnp.logical_not(safe))
    def _():
        m_ref[...] = jnp.full(m_ref.shape, -jnp.inf, F32)

        def body(j, carry):
            off = pl.multiple_of(j * tk, tk)
            kblk = k_ref[pl.ds(off, tk), :]
            for kvh in range(N_KV_HEADS):
                s = jnp.dot(kblk, qpad_ref[kvh], preferred_element_type=F32)
                m_old = m_ref[kvh]
                m_new = jnp.maximum(m_old, jnp.max(s, axis=0, keepdims=True))
                alpha = jnp.exp2(m_old - m_new)
                p = jnp.exp2(s - m_new).astype(BF16)
                vblk = vT_ref[kvh * V_ROWS:(kvh + 1) * V_ROWS, pl.ds(off, tk)]
                acc_ref[kvh] = acc_ref[kvh] * alpha + jnp.dot(vblk, p, preferred_element_type=F32)
                m_ref[kvh] = m_new
            return carry

        lax.fori_loop(0, jnp.where(is_latent, t // tk, ctx_len // tk), body, 0)

    for g in range(Q_PER_KV):
        og = jnp.concatenate(
            [acc_ref[kvh, 0:HEAD_DIM, g * tq:(g + 1) * tq]
             / acc_ref[kvh, HEAD_DIM:HEAD_DIM + 1, g * tq:(g + 1) * tq]
             for kvh in range(N_KV_HEADS)], axis=0)
        o_ref[:, g * KV_WIDTH:(g + 1) * KV_WIDTH] = og.T.astype(BF16)


def _attention(safe, qT, k, vT, n_ctx_tiles, ctx_len):
    batch, _, t = qT.shape
    tq = TOKEN_TILE
    tk = 256
    tk_main = 1024
    assert ctx_len % tk == 0 and t % tk == 0 and (t - ctx_len) % tk_main == 0
    kern = functools.partial(_attn_kernel, tq=tq, tk=tk, tk_main=tk_main, n_ctx_tiles=n_ctx_tiles,
                             ctx_len=ctx_len, t=t)
    return pl.pallas_call(
        kern,
        grid=(batch, t // tq),
        in_specs=[pl.BlockSpec(memory_space=pltpu.SMEM),
                  pl.BlockSpec((None, ATTN_WIDTH, tq), lambda b, i: (b, 0, i)),
                  pl.BlockSpec((None, t, KV_WIDTH), lambda b, i: (b, 0, 0)),
                  pl.BlockSpec((None, N_KV_HEADS * V_ROWS, t), lambda b, i: (b, 0, 0))],
        out_specs=pl.BlockSpec((None, tq, ATTN_WIDTH), lambda b, i: (b, i, 0)),
        out_shape=jax.ShapeDtypeStruct((batch, t, ATTN_WIDTH), BF16),
        scratch_shapes=[pltpu.VMEM((N_KV_HEADS, KV_WIDTH, Q_PER_KV * tq), BF16),
                        pltpu.VMEM((N_KV_HEADS, V_ROWS, Q_PER_KV * tq), F32),
                        pltpu.VMEM((N_KV_HEADS, 1, Q_PER_KV * tq), F32)],
        compiler_params=_cparams(2),
        name="attention",
    )(safe, qT, k, vT)


FFN_SPLIT = 1


def _mix_project(att_ref, gu_ref, vn_ref, sw_ref, sb_ref, wo_ref):
    tm = att_ref.shape[0]
    gd = GMLP_WIDTH // GMLP_GROUPS
    chunks = []
    for c in range(tm // GMLP_CHUNK):
        rows = slice(c * GMLP_CHUNK, (c + 1) * GMLP_CHUNK)
        s = jnp.concatenate(
            [jnp.dot(sw_ref[g], vn_ref[rows, g * gd:(g + 1) * gd], preferred_element_type=F32)
             for g in range(GMLP_GROUPS)], axis=1) + sb_ref[...]
        chunks.append(gu_ref[rows, :].astype(F32) * s)
    sg = jnp.concatenate(chunks, axis=0).astype(BF16)
    return (jnp.dot(att_ref[...], wo_ref[0:ATTN_WIDTH, :], preferred_element_type=F32)
            + jnp.dot(sg, wo_ref[ATTN_WIDTH:, :], preferred_element_type=F32))


def _ssm_project(yf_ref, yb_ref, xs_ref, z_ref, dsk_ref, ng_ref, wo_ref):
    gw = SSM_INNER // SSM_GROUPS
    out = None
    for g in range(SSM_GROUPS):
        cols = slice(g * gw, (g + 1) * gw)
        y = (yf_ref[:, cols].astype(F32) + yb_ref[:, cols].astype(F32)
             + dsk_ref[:, cols] * xs_ref[:, cols].astype(F32))
        y = y * _silu(z_ref[:, cols].astype(F32))
        ms = jnp.mean(y * y, axis=-1, keepdims=True)
        yn = (y * lax.rsqrt(ms + NORM_EPS) * ng_ref[:, cols]).astype(BF16)
        part = jnp.dot(yn, wo_ref[cols, :], preferred_element_type=F32)
        out = part if out is None else out + part
    return out


def _tail_kernel(*refs, project, n_tok, n_proj):
    tok_refs = refs[:n_tok]
    x_ref, mod_ref = refs[n_tok:n_tok + 2]
    proj_refs = refs[n_tok + 2:n_tok + 2 + n_proj]
    g2_ref, wi_ref, wf_ref, o_ref = refs[n_tok + 2 + n_proj:]
    mod = mod_ref[...]
    x = x_ref[...] + mod[:, 2 * D_MODEL:3 * D_MODEL] * project(*tok_refs, *proj_refs)
    h = _rms_mod(x, g2_ref[...], mod[:, 3 * D_MODEL:4 * D_MODEL], mod[:, 4 * D_MODEL:5 * D_MODEL]).astype(BF16)
    hc = FFN_HIDDEN // FFN_SPLIT
    acc = jnp.zeros(x.shape, F32)
    for c in range(FFN_SPLIT):
        g = jnp.dot(h, wi_ref[:, c * hc:(c + 1) * hc], preferred_element_type=F32)
        u = jnp.dot(h, wi_ref[:, FFN_HIDDEN + c * hc:FFN_HIDDEN + (c + 1) * hc], preferred_element_type=F32)
        a = (_silu(g) * u).astype(BF16)
        acc = acc + jnp.dot(a, wf_ref[c * hc:(c + 1) * hc, :], preferred_element_type=F32)
    o_ref[...] = x + mod[:, 5 * D_MODEL:6 * D_MODEL] * acc


def _layer_tail(project, name, tok_inputs, h, mods, proj_consts, ffn_consts, n_ctx_tiles, latent_only):
    batch, t, _ = h.shape
    tm = TOKEN_TILE
    skip = n_ctx_tiles if latent_only else 0
    tok = lambda n: pl.BlockSpec((None, tm, n), lambda b, i: (b, i + skip, 0))
    mod_spec = pl.BlockSpec((None, 1, 6 * D_MODEL),
                            lambda b, i: (jnp.where(i + skip < n_ctx_tiles, batch, b), 0, 0))

    def resident(c):
        if isinstance(c, tuple):
            return _layer_spec(*c, pipeline_mode=pl.Buffered(1))
        return pl.BlockSpec(c.shape, lambda b, i, nd=c.ndim: (0,) * nd, pipeline_mode=pl.Buffered(1))

    consts = list(proj_consts) + list(ffn_consts)
    arrays = [c[0] if isinstance(c, tuple) else c for c in consts]
    kern = functools.partial(_tail_kernel, project=project, n_tok=len(tok_inputs), n_proj=len(proj_consts))
    return pl.pallas_call(
        kern,
        grid=(batch, t // tm - skip),
        in_specs=([tok(a.shape[-1]) for a in tok_inputs] + [tok(D_MODEL), mod_spec]
                  + [resident(a) for a in consts]),
        out_specs=pl.BlockSpec((None, tm, D_MODEL), lambda b, i: (b, i, 0)),
        out_shape=jax.ShapeDtypeStruct((batch, t - skip * tm, D_MODEL), F32),
        compiler_params=_cparams(2),
        name=name,
    )(*tok_inputs, h, mods, *arrays)


def _ssm_in_kernel(xp_ref, x_ref, xn_ref, mod_ref, g1_ref, w_ref, wdt_ref, cw_ref, cb_ref, dtb_ref,
                   z_ref, xs_ref, bm_ref, cm_ref, dt_ref, *, n_tiles, n_ctx_tiles, col_block):
    i = pl.program_id(1)
    tm = x_ref.shape[0]
    halo = SUBLANES
    prev_ok = jnp.logical_and(i != 0, i != n_ctx_tiles)
    next_ok = jnp.logical_and(i != n_ctx_tiles - 1, i != n_tiles - 1)
    xcat = jnp.concatenate([x_ref[...], xn_ref[...], xp_ref[...]], axis=0)
    mod = mod_ref[...]
    h = _rms_mod(xcat, g1_ref[...], mod[:, 0:D_MODEL], mod[:, D_MODEL:2 * D_MODEL]).astype(BF16)
    hm = h[0:tm, :]
    z_ref[...] = jnp.dot(hm, w_ref[:, 0:SSM_INNER], preferred_element_type=F32).astype(BF16)
    dtr = jnp.dot(hm, wdt_ref[...], preferred_element_type=F32) + dtb_ref[...]
    dt_ref[...] = jnp.maximum(dtr, 0.0) + jnp.log1p(jnp.exp(-jnp.abs(dtr)))
    prev_f = prev_ok.astype(F32)
    next_f = next_ok.astype(F32)
    for blk in range(SSM_CONV_DIM // col_block):
        cols = slice(blk * col_block, (blk + 1) * col_block)
        p = jnp.dot(h, w_ref[:, SSM_INNER + blk * col_block:SSM_INNER + (blk + 1) * col_block],
                    preferred_element_type=F32)
        p = jnp.concatenate([p[0:tm, :], p[tm:tm + halo, :] * next_f, p[tm + halo:, :] * prev_f], axis=0)
        pm1 = pltpu.roll(p, 1, axis=0)[0:tm, :]
        pp1 = pltpu.roll(p, tm + 2 * halo - 1, axis=0)[0:tm, :]
        y = (cw_ref[0:1, cols] * pm1 + cw_ref[1:2, cols] * p[0:tm, :]
             + cw_ref[2:3, cols] * pp1 + cb_ref[:, cols])
        y = _silu(y).astype(BF16)
        c0 = blk * col_block
        if c0 < SSM_INNER:
            xs_ref[:, c0:c0 + col_block] = y
        elif c0 < SSM_INNER + SSM_BC_WIDTH:
            bm_ref[:, c0 - SSM_INNER:c0 - SSM_INNER + col_block] = y
        else:
            o0 = c0 - SSM_INNER - SSM_BC_WIDTH
            cm_ref[:, o0:o0 + col_block] = y


def _ssm_in(h, mods, g1, w_in_all, layer, wdt, cw, cb, dtb, n_ctx_tiles):
    batch, t, _ = h.shape
    tm = TOKEN_TILE
    n_tiles = t // tm
    per = tm // SUBLANES
    tok = lambda n: pl.BlockSpec((None, tm, n), lambda b, i: (b, i, 0))
    col_block = 512
    kern = functools.partial(_ssm_in_kernel, n_tiles=n_tiles, n_ctx_tiles=n_ctx_tiles, col_block=col_block)
    return pl.pallas_call(
        kern,
        grid=(batch, n_tiles),
        in_specs=[pl.BlockSpec((None, SUBLANES, D_MODEL), lambda b, i: (b, jnp.maximum(i * per - 1, 0), 0)),
                  tok(D_MODEL),
                  pl.BlockSpec((None, SUBLANES, D_MODEL),
                               lambda b, i: (b, jnp.minimum((i + 1) * per, t // SUBLANES - 1), 0)),
                  _mod_spec(n_ctx_tiles, batch), _const_spec((1, D_MODEL)),
                  _layer_spec(w_in_all, layer),
                  _const_spec((D_MODEL, LANES)), _const_spec((SUBLANES, SSM_CONV_DIM)),
                  _const_spec((1, SSM_CONV_DIM)), _const_spec((1, LANES))],
        out_specs=[tok(SSM_INNER), tok(SSM_INNER), tok(SSM_BC_WIDTH), tok(SSM_BC_WIDTH), tok(LANES)],
        out_shape=[jax.ShapeDtypeStruct((batch, t, SSM_INNER), BF16),
                   jax.ShapeDtypeStruct((batch, t, SSM_INNER), BF16),
                   jax.ShapeDtypeStruct((batch, t, SSM_BC_WIDTH), BF16),
                   jax.ShapeDtypeStruct((batch, t, SSM_BC_WIDTH), BF16),
                   jax.ShapeDtypeStruct((batch, t, LANES), F32)],
        compiler_params=_cparams(2),
        name="ssm_in",
    )(h, h, h, mods, g1, w_in_all, wdt, cw, cb, dtb)


def _ssd_chunk(xs_ref, bm_ref, cm_ref, dt_ref, a_ref, ex_ref, y_ref, st_ref, *, reverse, lane_base):
    L = SSM_CHUNK

    @pl.when(pl.program_id(1) == 0)
    def _():
        st_ref[...] = jnp.zeros(st_ref.shape, F32)

    dt = dt_ref[...]
    da = dt * a_ref[...]
    r = lax.broadcasted_iota(jnp.int32, (L, L), 0)
    c = lax.broadcasted_iota(jnp.int32, (L, L), 1)
    before = (c >= r) if reverse else (c <= r)
    tri = jnp.where(before, 1.0, 0.0).astype(BF16)
    hi = da.astype(BF16)
    r1 = da - hi.astype(F32)
    mid = r1.astype(BF16)
    lo = (r1 - mid.astype(F32)).astype(BF16)
    p3 = jnp.dot(tri, jnp.concatenate([hi, mid, lo], axis=1), preferred_element_type=F32)
    P = p3[:, 0:LANES] + p3[:, LANES:2 * LANES] + p3[:, 2 * LANES:3 * LANES]
    tot = jnp.sum(da, axis=0, keepdims=True)
    P2 = P * LOG2E
    Q2T = ((P - jnp.log(dt)) * LOG2E).T

    def expand(v):
        return jnp.dot(jnp.concatenate(_split2(v), axis=1), ex_ref[...], preferred_element_type=F32)

    w_state_all = expand(dt * jnp.exp(tot - P))
    e_p_all = expand(jnp.exp(P))
    end = 0 if reverse else L - 1
    lane = lax.broadcasted_iota(jnp.int32, (1, LANES), 1)
    first_head = jnp.where(lane < SSM_HEAD_DIM, 1.0, 0.0).astype(BF16)
    second_head = jnp.where(lane >= SSM_HEAD_DIM, 1.0, 0.0).astype(BF16)
    heads_per_group = SSM_HEADS // SSM_GROUPS
    gw = heads_per_group * SSM_HEAD_DIM
    for g in range(SSM_GROUPS):
        e_p_x = e_p_all[:, g * gw:(g + 1) * gw]
        bg = bm_ref[:, g * SSM_STATE:(g + 1) * SSM_STATE]
        cg = cm_ref[:, g * SSM_STATE:(g + 1) * SSM_STATE]
        cb = lax.dot_general(cg, bg, (((1,), (1,)), ((), ())), preferred_element_type=F32)
        st = st_ref[g]
        y_off = jnp.dot(cg, st.astype(BF16), preferred_element_type=F32) * e_p_x
        ys = []
        for pair in range(heads_per_group // 2):
            ms = []
            for k in range(2):
                hl = lane_base + g * heads_per_group + 2 * pair + k
                seg = P2[:, hl:hl + 1] - Q2T[hl:hl + 1, :]
                ms.append((cb * jnp.exp2(jnp.where(before, seg, -jnp.inf))).astype(BF16))
            xp = xs_ref[:, g * gw + pair * LANES:g * gw + (pair + 1) * LANES]
            rhs = jnp.concatenate([xp * first_head, xp * second_head], axis=0)
            ys.append(jnp.dot(jnp.concatenate(ms, axis=1), rhs, preferred_element_type=F32))
        y_ref[:, g * gw:(g + 1) * gw] = (jnp.concatenate(ys, axis=1) + y_off).astype(BF16)
        xdd = xs_ref[:, g * gw:(g + 1) * gw] * w_state_all[:, g * gw:(g + 1) * gw].astype(BF16)
        bgT = bg.astype(F32).T.astype(BF16)
        st_ref[g] = st * e_p_x[end:end + 1, :] + jnp.dot(bgT, xdd, preferred_element_type=F32)


def _ssd_pair_kernel(xf_ref, bf_ref, cf_ref, dtf_ref, xb_ref, bb_ref, cb_ref, dtb_ref, a_ref, ex_ref,
                     yf_ref, yb_ref, stf_ref, stb_ref):
    _ssd_chunk(xf_ref, bf_ref, cf_ref, dtf_ref, a_ref.at[0], ex_ref.at[0], yf_ref, stf_ref,
               reverse=False, lane_base=0)
    _ssd_chunk(xb_ref, bb_ref, cb_ref, dtb_ref, a_ref.at[1], ex_ref.at[1], yb_ref, stb_ref,
               reverse=True, lane_base=SSM_HEADS)


def _ssd(xs, bm, cm, dt, a_rows, ex, n_ctx_chunks):
    batch, t, _ = xs.shape
    L = SSM_CHUNK
    nc = t // L
    fwd = lambda i: i
    bwd = lambda i: jnp.where(i < n_ctx_chunks, n_ctx_chunks - 1 - i, nc - 1 + n_ctx_chunks - i)
    tok = lambda n, cmap: pl.BlockSpec((None, L, n), lambda b, i: (b, cmap(i), 0))
    chunk_specs = lambda cmap: [tok(SSM_INNER, cmap), tok(SSM_BC_WIDTH, cmap), tok(SSM_BC_WIDTH, cmap),
                                tok(LANES, cmap)]
    state = pltpu.VMEM((SSM_GROUPS, SSM_STATE, SSM_INNER // SSM_GROUPS), F32)
    y_shape = jax.ShapeDtypeStruct((batch, t, SSM_INNER), BF16)
    return pl.pallas_call(
        _ssd_pair_kernel,
        grid=(batch, nc),
        in_specs=(chunk_specs(fwd) + chunk_specs(bwd)
                  + [_const_spec((2, 1, LANES)), _const_spec((2, 2 * LANES, SSM_INNER))]),
        out_specs=[tok(SSM_INNER, fwd), tok(SSM_INNER, bwd)],
        out_shape=[y_shape, y_shape],
        scratch_shapes=[state, state],
        compiler_params=_cparams(2),
        name="ssd",
    )(xs, bm, cm, dt, xs, bm, cm, dt, a_rows, ex)


def _rope_tables(ctx_len, seq):
    tpos = jnp.arange(seq)
    row = (tpos // GRID_W).astype(F32)
    col = (tpos % GRID_W).astype(F32)
    n_freq = HEAD_DIM // 4
    inv = ROPE_THETA ** (-jnp.arange(n_freq, dtype=F32) / n_freq)
    ang = jnp.concatenate([row[:, None] * inv, col[:, None] * inv], axis=-1)
    cos = jnp.cos(ang)
    sin = jnp.sin(ang)
    cos64 = jnp.concatenate([cos, cos], axis=-1)
    sin64 = jnp.concatenate([-sin, sin], axis=-1)
    cos_t = jnp.concatenate([jnp.ones((ctx_len, HEAD_DIM), F32), cos64], axis=0)
    sin_t = jnp.concatenate([jnp.zeros((ctx_len, HEAD_DIM), F32), sin64], axis=0)
    return jnp.tile(cos_t, (1, N_KV_HEADS)), jnp.tile(sin_t, (1, N_KV_HEADS))


def _block_mean_matrix(n, group):
    idx = np.arange(n) // group
    return jnp.asarray((idx[:, None] == idx[None, :]).astype(np.float32) / group, dtype=BF16)


def kernel(x, c, ctx, c_ctx, ada_w, ada_b, norm1_g, norm2_g, ffn_w_in, ffn_w_out, mix_w_in, mix_w_out,
           q_norm_g, k_norm_g, sgu_norm_g, sgu_w, sgu_b, ssm_w_in, ssm_conv_w, ssm_conv_b, ssm_dt_bias,
           ssm_a_log, ssm_d, ssm_norm_g, ssm_w_out):
    batch, seq, _ = x.shape
    ctx_len = ctx.shape[1]
    depth = ada_w.shape[0]
    assert ctx_len % TOKEN_TILE == 0 and seq % TOKEN_TILE == 0 and batch < SUBLANES
    n_ctx_tiles = ctx_len // TOKEN_TILE
    n_ctx_chunks = ctx_len // SSM_CHUNK
    assert ssm_conv_w.shape[1] == 3

    h = jnp.concatenate([ctx, x], axis=1)
    cond = jnp.zeros((SUBLANES, D_MODEL), F32).at[:batch].set(c).at[batch].set(c_ctx)
    mods = _adaln_all(cond, ada_w, ada_b)[:, :, None, :]

    cos_t, sin_t = _rope_tables(ctx_len, seq)
    gq = _block_mean_matrix(ATTN_WIDTH, HEAD_DIM)
    gk = _block_mean_matrix(KV_WIDTH, HEAD_DIM)
    perm = np.arange(ATTN_WIDTH).reshape(N_KV_HEADS, Q_PER_KV, HEAD_DIM).transpose(1, 0, 2).reshape(-1)
    ex_np = np.zeros((2, 2 * LANES, SSM_INNER), np.float32)
    for d in range(2):
        for hd in range(SSM_HEADS):
            for half in range(2):
                ex_np[d, half * LANES + d * SSM_HEADS + hd, hd * SSM_HEAD_DIM:(hd + 1) * SSM_HEAD_DIM] = 1.0
    ex = jnp.asarray(ex_np, dtype=BF16)

    ffn_w_in_b = ffn_w_in.astype(BF16)
    ffn_w_out_b = ffn_w_out.astype(BF16)
    mix_w_in_b = mix_w_in.astype(BF16)
    ssm_w_in_b = ssm_w_in.astype(BF16)
    ssm_w_out_b = ssm_w_out.astype(BF16)
    norm2_rows = norm2_g[:, None, :]
    for i in range(depth):
        j = i // 2
        m = mods[i]
        g1 = norm1_g[i][None, :]
        last = i == depth - 1
        ffn_consts = [(norm2_rows, i), (ffn_w_in_b, i), (ffn_w_out_b, i)]
        if i % 2 == 0:
            qT, k, vT, gu, vn = _mix_in(
                h, m, g1, mix_w_in_b, j, jnp.tile(q_norm_g[j], N_Q_HEADS)[None, :],
                jnp.tile(k_norm_g[j], N_KV_HEADS)[None, :], cos_t, sin_t, gq, gk,
                sgu_norm_g[j][None, :], n_ctx_tiles)
            score_bound = (1.02 * HEAD_DIM ** 0.5 * LOG2E) * jnp.max(jnp.abs(q_norm_g[j])) * jnp.max(jnp.abs(k_norm_g[j]))
            safe = (score_bound <= SAFE_LOG2_BOUND).astype(jnp.int32).reshape(1)
            att = _attention(safe, qT, k, vT, n_ctx_tiles, ctx_len)
            w_out = mix_w_out[j]
            w_out = jnp.concatenate([w_out[:ATTN_WIDTH][perm], w_out[ATTN_WIDTH:]], axis=0).astype(BF16)
            sb = jnp.repeat(sgu_b[j].T, GMLP_WIDTH // GMLP_GROUPS, axis=1)
            h = _layer_tail(_mix_project, "mix_tail", [att, gu, vn], h, m,
                            [sgu_w[j].astype(BF16), sb, w_out], ffn_consts, n_ctx_tiles, last)
        else:
            n_dt = 2 * SSM_HEADS
            wdt = jnp.zeros((D_MODEL, LANES), BF16).at[:, :n_dt].set(ssm_w_in_b[j, :, SSM_INNER + SSM_CONV_DIM:])
            dtb = jnp.zeros((1, LANES), F32).at[0, :n_dt].set(ssm_dt_bias[j].reshape(-1))
            cw = jnp.zeros((SUBLANES, SSM_CONV_DIM), F32).at[:ssm_conv_w.shape[1]].set(ssm_conv_w[j])
            z, xs, bm, cm, dt = _ssm_in(h, m, g1, ssm_w_in_b, j, wdt, cw, ssm_conv_b[j][None, :], dtb, n_ctx_tiles)
            a = -jnp.exp(ssm_a_log[j].astype(F32))
            a_rows = jnp.zeros((2, 1, LANES), F32)
            for d in range(2):
                a_rows = a_rows.at[d, 0, d * SSM_HEADS:(d + 1) * SSM_HEADS].set(a[d])
            ys = _ssd(xs, bm, cm, dt, a_rows, ex, n_ctx_chunks)
            dsk = jnp.repeat(ssm_d[j], SSM_HEAD_DIM)[None, :]
            h = _layer_tail(_ssm_project, "ssm_tail", [ys[0], ys[1], xs, z], h, m,
                            [dsk, ssm_norm_g[j][None, :], (ssm_w_out_b, j)], ffn_consts,
                            n_ctx_tiles, last)
    return h
```

```python
import functools
import math

import jax
import jax.numpy as jnp
import numpy as np
from jax import lax
from jax.experimental import pallas as pl
from jax.experimental.pallas import tpu as pltpu

F32 = jnp.float32
BF16 = jnp.bfloat16

D_MODEL = 1024
GRID_W = 64
HEAD_DIM = 64
ATTN_WIDTH = D_MODEL // 2
N_Q_HEADS = ATTN_WIDTH // HEAD_DIM
N_KV_HEADS = 2
Q_PER_KV = N_Q_HEADS // N_KV_HEADS
KV_WIDTH = N_KV_HEADS * HEAD_DIM
ROPE_THETA = 10000.0
GMLP_WIDTH = D_MODEL // 2
GMLP_CHUNK = 128
GMLP_GROUPS = 4
MIX_IN = ATTN_WIDTH + 2 * KV_WIDTH + 2 * GMLP_WIDTH
SSM_INNER = 2 * D_MODEL
SSM_HEAD_DIM = 64
SSM_HEADS = SSM_INNER // SSM_HEAD_DIM
SSM_GROUPS = 4
SSM_STATE = 128
SSM_CHUNK = 128
SSM_BC_WIDTH = SSM_GROUPS * SSM_STATE
SSM_CONV_DIM = SSM_INNER + 2 * SSM_BC_WIDTH
FFN_HIDDEN = -(-8 * D_MODEL // (3 * 256)) * 256
NORM_EPS = 1e-6

LANES = 128
SUBLANES = 8
TOKEN_TILE = 256
VMEM_LIMIT = 56 * 1024 * 1024
LOG2E = 1.4426950408889634
BF16_ROWS = 16
V_ROWS = HEAD_DIM + BF16_ROWS
SAFE_LOG2_BOUND = 60.0


def _cparams(n_axes):
    return pltpu.CompilerParams(dimension_semantics=("arbitrary",) * n_axes,
                                vmem_limit_bytes=VMEM_LIMIT)


def _silu(x):
    return x / (1.0 + jnp.exp2(x * (-LOG2E)))


def _gelu_tanh(x):
    c = math.sqrt(2.0 / math.pi)
    return x * (0.5 * (1.0 + jnp.tanh(c * (x + 0.044715 * (x * x * x)))))


def _rms_mod(x, g, shift, scale):
    ms = jnp.mean(x * x, axis=-1, keepdims=True)
    return (x * lax.rsqrt(ms + NORM_EPS)) * (g * (1.0 + scale)) + shift


def _split2(x):
    hi = x.astype(BF16)
    lo = (x - hi.astype(F32)).astype(BF16)
    return hi, lo


def _group_mean(x2, gmat):
    hi, lo = _split2(x2)
    return (jnp.dot(hi, gmat, preferred_element_type=F32)
            + jnp.dot(lo, gmat, preferred_element_type=F32))


def _rope(x, cos, sin_signed):
    n = x.shape[-1]
    half = HEAD_DIM // 2
    fwd = pltpu.roll(x, n - half, axis=1)
    bwd = pltpu.roll(x, half, axis=1)
    lane = lax.broadcasted_iota(jnp.int32, x.shape, 1)
    partner = jnp.where((lane % HEAD_DIM) < half, fwd, bwd)
    return x * cos + partner * sin_signed


def _adaln_kernel(c_ref, w_ref, b_ref, o_ref):
    s = _silu(c_ref[...]).astype(BF16)
    o_ref[...] = jnp.dot(s, w_ref[...].astype(BF16), preferred_element_type=F32) + b_ref[...]


def _adaln_all(cond, ada_w, ada_b):
    depth = ada_w.shape[0]
    tn = 1536
    return pl.pallas_call(
        _adaln_kernel,
        grid=(depth, 6 * D_MODEL // tn),
        in_specs=[pl.BlockSpec((SUBLANES, D_MODEL), lambda l, j: (0, 0)),
                  pl.BlockSpec((None, D_MODEL, tn), lambda l, j: (l, 0, j)),
                  pl.BlockSpec((None, 1, tn), lambda l, j: (l, 0, j))],
        out_specs=pl.BlockSpec((None, SUBLANES, tn), lambda l, j: (l, 0, j)),
        out_shape=jax.ShapeDtypeStruct((depth, SUBLANES, 6 * D_MODEL), F32),
        compiler_params=_cparams(2),
        name="adaln",
    )(cond, ada_w, ada_b.reshape(depth, 1, 6 * D_MODEL))


def _mod_spec(n_ctx_tiles, batch):
    return pl.BlockSpec((None, 1, 6 * D_MODEL),
                        lambda b, i: (jnp.where(i < n_ctx_tiles, batch, b), 0, 0))


def _const_spec(shape):
    nd = len(shape)
    return pl.BlockSpec(shape, lambda b, i: (0,) * nd)


def _layer_spec(stacked, layer, **kwargs):
    idx = (layer,) + (0,) * (stacked.ndim - 1)
    return pl.BlockSpec((None,) + stacked.shape[1:], lambda b, i: idx, **kwargs)


def _mix_in_kernel(x_ref, mod_ref, g1_ref, w_ref, qg_ref, kg_ref, cos_ref, sin_ref, gq_ref, gk_ref,
                   sg_ref, qT_ref, k_ref, vT_ref, gu_ref, vn_ref):
    mod = mod_ref[...]
    h = _rms_mod(x_ref[...], g1_ref[...], mod[:, 0:D_MODEL], mod[:, D_MODEL:2 * D_MODEL]).astype(BF16)
    p = jnp.dot(h, w_ref[...], preferred_element_type=F32)
    q = p[:, 0:ATTN_WIDTH]
    k = p[:, ATTN_WIDTH:ATTN_WIDTH + KV_WIDTH]
    v = p[:, ATTN_WIDTH + KV_WIDTH:ATTN_WIDTH + 2 * KV_WIDTH]
    gm = p[:, ATTN_WIDTH + 2 * KV_WIDTH:]
    cos = cos_ref[...]
    sin = sin_ref[...]
    q = q * lax.rsqrt(_group_mean(q * q, gq_ref[...]) + NORM_EPS) * qg_ref[...]
    k = k * lax.rsqrt(_group_mean(k * k, gk_ref[...]) + NORM_EPS) * kg_ref[...]
    reps = ATTN_WIDTH // KV_WIDTH
    q = _rope(q, jnp.concatenate([cos] * reps, axis=1), jnp.concatenate([sin] * reps, axis=1))
    k = _rope(k, cos, sin)
    q = q * (HEAD_DIM ** -0.5 * LOG2E)
    qT_ref[...] = q.T.astype(BF16)
    k_ref[...] = k.astype(BF16)
    vT = v.T
    ones = jnp.ones((BF16_ROWS, vT.shape[1]), F32)
    vT_ref[...] = jnp.concatenate(
        [blk for kvh in range(N_KV_HEADS) for blk in (vT[kvh * HEAD_DIM:(kvh + 1) * HEAD_DIM, :], ones)],
        axis=0).astype(BF16)
    ge = _gelu_tanh(gm)
    gu_ref[...] = ge[:, 0:GMLP_WIDTH].astype(BF16)
    gv = ge[:, GMLP_WIDTH:]
    gd = GMLP_WIDTH // GMLP_GROUPS
    parts = []
    for g in range(GMLP_GROUPS):
        blk = gv[:, g * gd:(g + 1) * gd]
        ms = jnp.mean(blk * blk, axis=-1, keepdims=True)
        parts.append(blk * lax.rsqrt(ms + NORM_EPS))
    vn_ref[...] = (jnp.concatenate(parts, axis=1) * sg_ref[...]).astype(BF16)


def _mix_in(h, mods, g1, w_in_all, layer, qg, kg, cos, sin, gq, gk, sgg, n_ctx_tiles):
    batch, t, _ = h.shape
    tm = TOKEN_TILE
    tok = lambda n: pl.BlockSpec((None, tm, n), lambda b, i: (b, i, 0))
    tr = lambda n: pl.BlockSpec((None, n, tm), lambda b, i: (b, 0, i))
    return pl.pallas_call(
        _mix_in_kernel,
        grid=(batch, t // tm),
        in_specs=[tok(D_MODEL), _mod_spec(n_ctx_tiles, batch), _const_spec((1, D_MODEL)),
                  _layer_spec(w_in_all, layer), _const_spec((1, ATTN_WIDTH)), _const_spec((1, KV_WIDTH)),
                  pl.BlockSpec((tm, KV_WIDTH), lambda b, i: (i, 0)),
                  pl.BlockSpec((tm, KV_WIDTH), lambda b, i: (i, 0)),
                  _const_spec((ATTN_WIDTH, ATTN_WIDTH)), _const_spec((KV_WIDTH, KV_WIDTH)),
                  _const_spec((1, GMLP_WIDTH))],
        out_specs=[tr(ATTN_WIDTH), tok(KV_WIDTH), tr(N_KV_HEADS * V_ROWS), tok(GMLP_WIDTH), tok(GMLP_WIDTH)],
        out_shape=[jax.ShapeDtypeStruct((batch, ATTN_WIDTH, t), BF16),
                   jax.ShapeDtypeStruct((batch, t, KV_WIDTH), BF16),
                   jax.ShapeDtypeStruct((batch, N_KV_HEADS * V_ROWS, t), BF16),
                   jax.ShapeDtypeStruct((batch, t, GMLP_WIDTH), BF16),
                   jax.ShapeDtypeStruct((batch, t, GMLP_WIDTH), BF16)],
        compiler_params=_cparams(2),
        name="mix_in",
    )(h, mods, g1, w_in_all, qg, kg, cos, sin, gq, gk, sgg)


def _attn_kernel(safe_ref, qT_ref, k_ref, vT_ref, o_ref, qpad_ref, acc_ref, m_ref, *, tq, tk, tk_main,
                 n_ctx_tiles, ctx_len, t):
    i = pl.program_id(1)
    for kvh in range(N_KV_HEADS):
        qcat = jnp.concatenate(
            [qT_ref[(kvh * Q_PER_KV + g) * HEAD_DIM:(kvh * Q_PER_KV + g + 1) * HEAD_DIM, :]
             for g in range(Q_PER_KV)], axis=1)
        z = jnp.zeros_like(qcat)
        qpad_ref[kvh] = jnp.concatenate([qcat, z] if kvh == 0 else [z, qcat], axis=0)
    acc_ref[...] = jnp.zeros(acc_ref.shape, F32)
    is_latent = i >= n_ctx_tiles
    safe = safe_ref[0] != 0

    def plain_step(off, width):
        kblk = k_ref[pl.ds(off, width), :]
        for kvh in range(N_KV_HEADS):
            s = jnp.dot(kblk, qpad_ref[kvh], preferred_element_type=F32)
            p = jnp.exp2(s).astype(BF16)
            vblk = vT_ref[kvh * V_ROWS:(kvh + 1) * V_ROWS, pl.ds(off, width)]
            acc_ref[kvh] += jnp.dot(vblk, p, preferred_element_type=F32)

    @pl.when(safe)
    def _():
        plain_step(0, ctx_len)

        @pl.when(is_latent)
        def _():
            def body(j, carry):
                plain_step(pl.multiple_of(ctx_len + j * tk_main, LANES), tk_main)
                return carry
            lax.fori_loop(0, (t - ctx_len) // tk_main, body, 0, unroll=4)

    @pl.when(jnp.logical_not(safe))
    def _():
        m_ref[...] = jnp.full(m_ref.shape, -jnp.inf, F32)

        def body(j, carry):
            off = pl.multiple_of(j * tk, tk)
            kblk = k_ref[pl.ds(off, tk), :]
            for kvh in range(N_KV_HEADS):
                s = jnp.dot(kblk, qpad_ref[kvh], preferred_element_type=F32)
                m_old = m_ref[kvh]
                m_new = jnp.maximum(m_old, jnp.max(s, axis=0, keepdims=True))
                alpha = jnp.exp2(m_old - m_new)
                p = jnp.exp2(s - m_new).astype(BF16)
                vblk = vT_ref[kvh * V_ROWS:(kvh + 1) * V_ROWS, pl.ds(off, tk)]
                acc_ref[kvh] = acc_ref[kvh] * alpha + jnp.dot(vblk, p, preferred_element_type=F32)
                m_ref[kvh] = m_new
            return carry

        lax.fori_loop(0, jnp.where(is_latent, t // tk, ctx_len // tk), body, 0)

    for g in range(Q_PER_KV):
        og = jnp.concatenate(
            [acc_ref[kvh, 0:HEAD_DIM, g * tq:(g + 1) * tq]
             / acc_ref[kvh, HEAD_DIM:HEAD_DIM + 1, g * tq:(g + 1) * tq]
             for kvh in range(N_KV_HEADS)], axis=0)
        o_ref[:, g * KV_WIDTH:(g + 1) * KV_WIDTH] = og.T.astype(BF16)


def _attention(safe, qT, k, vT, n_ctx_tiles, ctx_len):
    batch, _, t = qT.shape
    tq = TOKEN_TILE
    tk = 256
    tk_main = 2048
    assert ctx_len % tk == 0 and t % tk == 0 and (t - ctx_len) % tk_main == 0
    kern = functools.partial(_attn_kernel, tq=tq, tk=tk, tk_main=tk_main, n_ctx_tiles=n_ctx_tiles,
                             ctx_len=ctx_len, t=t)
    return pl.pallas_call(
        kern,
        grid=(batch, t // tq),
        in_specs=[pl.BlockSpec(memory_space=pltpu.SMEM),
                  pl.BlockSpec((None, ATTN_WIDTH, tq), lambda b, i: (b, 0, i)),
                  pl.BlockSpec((None, t, KV_WIDTH), lambda b, i: (b, 0, 0)),
                  pl.BlockSpec((None, N_KV_HEADS * V_ROWS, t), lambda b, i: (b, 0, 0))],
        out_specs=pl.BlockSpec((None, tq, ATTN_WIDTH), lambda b, i: (b, i, 0)),
        out_shape=jax.ShapeDtypeStruct((batch, t, ATTN_WIDTH), BF16),
        scratch_shapes=[pltpu.VMEM((N_KV_HEADS, KV_WIDTH, Q_PER_KV * tq), BF16),
                        pltpu.VMEM((N_KV_HEADS, V_ROWS, Q_PER_KV * tq), F32),
                        pltpu.VMEM((N_KV_HEADS, 1, Q_PER_KV * tq), F32)],
        compiler_params=_cparams(2),
        name="attention",
    )(safe, qT, k, vT)


FFN_SPLIT = 1


def _mix_project(att_ref, gu_ref, vn_ref, sw_ref, sb_ref, wo_ref):
    tm = att_ref.shape[0]
    gd = GMLP_WIDTH // GMLP_GROUPS
    chunks = []
    for c in range(tm // GMLP_CHUNK):
        rows = slice(c * GMLP_CHUNK, (c + 1) * GMLP_CHUNK)
        s = jnp.concatenate(
            [jnp.dot(sw_ref[g], vn_ref[rows, g * gd:(g + 1) * gd], preferred_element_type=F32)
             for g in range(GMLP_GROUPS)], axis=1) + sb_ref[...]
        chunks.append(gu_ref[rows, :].astype(F32) * s)
    sg = jnp.concatenate(chunks, axis=0).astype(BF16)
    return (jnp.dot(att_ref[...], wo_ref[0:ATTN_WIDTH, :], preferred_element_type=F32)
            + jnp.dot(sg, wo_ref[ATTN_WIDTH:, :], preferred_element_type=F32))


def _ssm_project(yf_ref, yb_ref, xs_ref, z_ref, dsk_ref, ng_ref, wo_ref):
    gw = SSM_INNER // SSM_GROUPS
    out = None
    for g in range(SSM_GROUPS):
        cols = slice(g * gw, (g + 1) * gw)
        y = (yf_ref[:, cols].astype(F32) + yb_ref[:, cols].astype(F32)
             + dsk_ref[:, cols] * xs_ref[:, cols].astype(F32))
        y = y * _silu(z_ref[:, cols].astype(F32))
        ms = jnp.mean(y * y, axis=-1, keepdims=True)
        yn = (y * lax.rsqrt(ms + NORM_EPS) * ng_ref[:, cols]).astype(BF16)
        part = jnp.dot(yn, wo_ref[cols, :], preferred_element_type=F32)
        out = part if out is None else out + part
    return out


def _tail_kernel(*refs, project, n_tok, n_proj):
    tok_refs = refs[:n_tok]
    x_ref, mod_ref = refs[n_tok:n_tok + 2]
    proj_refs = refs[n_tok + 2:n_tok + 2 + n_proj]
    g2_ref, wi_ref, wf_ref, o_ref = refs[n_tok + 2 + n_proj:]
    mod = mod_ref[...]
    x = x_ref[...] + mod[:, 2 * D_MODEL:3 * D_MODEL] * project(*tok_refs, *proj_refs)
    h = _rms_mod(x, g2_ref[...], mod[:, 3 * D_MODEL:4 * D_MODEL], mod[:, 4 * D_MODEL:5 * D_MODEL]).astype(BF16)
    hc = FFN_HIDDEN // FFN_SPLIT
    acc = jnp.zeros(x.shape, F32)
    for c in range(FFN_SPLIT):
        g = jnp.dot(h, wi_ref[:, c * hc:(c + 1) * hc], preferred_element_type=F32)
        u = jnp.dot(h, wi_ref[:, FFN_HIDDEN + c * hc:FFN_HIDDEN + (c + 1) * hc], preferred_element_type=F32)
        a = (_silu(g) * u).astype(BF16)
        acc = acc + jnp.dot(a, wf_ref[c * hc:(c + 1) * hc, :], preferred_element_type=F32)
    o_ref[...] = x + mod[:, 5 * D_MODEL:6 * D_MODEL] * acc


def _layer_tail(project, name, tok_inputs, h, mods, proj_consts, ffn_consts, n_ctx_tiles, latent_only):
    batch, t, _ = h.shape
    tm = TOKEN_TILE
    skip = n_ctx_tiles if latent_only else 0
    tok = lambda n: pl.BlockSpec((None, tm, n), lambda b, i: (b, i + skip, 0))
    mod_spec = pl.BlockSpec((None, 1, 6 * D_MODEL),
                            lambda b, i: (jnp.where(i + skip < n_ctx_tiles, batch, b), 0, 0))

    def resident(c):
        if isinstance(c, tuple):
            return _layer_spec(*c, pipeline_mode=pl.Buffered(1))
        return pl.BlockSpec(c.shape, lambda b, i, nd=c.ndim: (0,) * nd, pipeline_mode=pl.Buffered(1))

    consts = list(proj_consts) + list(ffn_consts)
    arrays = [c[0] if isinstance(c, tuple) else c for c in consts]
    kern = functools.partial(_tail_kernel, project=project, n_tok=len(tok_inputs), n_proj=len(proj_consts))
    return pl.pallas_call(
        kern,
        grid=(batch, t // tm - skip),
        in_specs=([tok(a.shape[-1]) for a in tok_inputs] + [tok(D_MODEL), mod_spec]
                  + [resident(a) for a in consts]),
        out_specs=pl.BlockSpec((None, tm, D_MODEL), lambda b, i: (b, i, 0)),
        out_shape=jax.ShapeDtypeStruct((batch, t - skip * tm, D_MODEL), F32),
        compiler_params=_cparams(2),
        name=name,
    )(*tok_inputs, h, mods, *arrays)


def _ssm_in_kernel(xp_ref, x_ref, xn_ref, mod_ref, g1_ref, w_ref, wdt_ref, cw_ref, cb_ref, dtb_ref,
                   z_ref, xs_ref, bm_ref, cm_ref, dt_ref, *, n_tiles, n_ctx_tiles, col_block):
    i = pl.program_id(1)
    tm = x_ref.shape[0]
    halo = SUBLANES
    prev_ok = jnp.logical_and(i != 0, i != n_ctx_tiles)
    next_ok = jnp.logical_and(i != n_ctx_tiles - 1, i != n_tiles - 1)
    xcat = jnp.concatenate([x_ref[...], xn_ref[...], xp_ref[...]], axis=0)
    mod = mod_ref[...]
    h = _rms_mod(xcat, g1_ref[...], mod[:, 0:D_MODEL], mod[:, D_MODEL:2 * D_MODEL]).astype(BF16)
    hm = h[0:tm, :]
    z_ref[...] = jnp.dot(hm, w_ref[:, 0:SSM_INNER], preferred_element_type=F32).astype(BF16)
    dtr = jnp.dot(hm, wdt_ref[...], preferred_element_type=F32) + dtb_ref[...]
    dt_ref[...] = jnp.maximum(dtr, 0.0) + jnp.log1p(jnp.exp(-jnp.abs(dtr)))
    prev_f = prev_ok.astype(F32)
    next_f = next_ok.astype(F32)
    for blk in range(SSM_CONV_DIM // col_block):
        cols = slice(blk * col_block, (blk + 1) * col_block)
        p = jnp.dot(h, w_ref[:, SSM_INNER + blk * col_block:SSM_INNER + (blk + 1) * col_block],
                    preferred_element_type=F32)
        p = jnp.concatenate([p[0:tm, :], p[tm:tm + halo, :] * next_f, p[tm + halo:, :] * prev_f], axis=0)
        pm1 = pltpu.roll(p, 1, axis=0)[0:tm, :]
        pp1 = pltpu.roll(p, tm + 2 * halo - 1, axis=0)[0:tm, :]
        y = (cw_ref[0:1, cols] * pm1 + cw_ref[1:2, cols] * p[0:tm, :]
             + cw_ref[2:3, cols] * pp1 + cb_ref[:, cols])
        y = _silu(y).astype(BF16)
        c0 = blk * col_block
        if c0 < SSM_INNER:
            xs_ref[:, c0:c0 + col_block] = y
        elif c0 < SSM_INNER + SSM_BC_WIDTH:
            bm_ref[:, c0 - SSM_INNER:c0 - SSM_INNER + col_block] = y
        else:
            o0 = c0 - SSM_INNER - SSM_BC_WIDTH
            cm_ref[:, o0:o0 + col_block] = y


def _ssm_in(h, mods, g1, w_in_all, layer, wdt, cw, cb, dtb, n_ctx_tiles):
    batch, t, _ = h.shape
    tm = TOKEN_TILE
    n_tiles = t // tm
    per = tm // SUBLANES
    tok = lambda n: pl.BlockSpec((None, tm, n), lambda b, i: (b, i, 0))
    col_block = 512
    kern = functools.partial(_ssm_in_kernel, n_tiles=n_tiles, n_ctx_tiles=n_ctx_tiles, col_block=col_block)
    return pl.pallas_call(
        kern,
        grid=(batch, n_tiles),
        in_specs=[pl.BlockSpec((None, SUBLANES, D_MODEL), lambda b, i: (b, jnp.maximum(i * per - 1, 0), 0)),
                  tok(D_MODEL),
                  pl.BlockSpec((None, SUBLANES, D_MODEL),
                               lambda b, i: (b, jnp.minimum((i + 1) * per, t // SUBLANES - 1), 0)),
                  _mod_spec(n_ctx_tiles, batch), _const_spec((1, D_MODEL)),
                  _layer_spec(w_in_all, layer),
                  _const_spec((D_MODEL, LANES)), _const_spec((SUBLANES, SSM_CONV_DIM)),
                  _const_spec((1, SSM_CONV_DIM)), _const_spec((1, LANES))],
        out_specs=[tok(SSM_INNER), tok(SSM_INNER), tok(SSM_BC_WIDTH), tok(SSM_BC_WIDTH), tok(LANES)],
        out_shape=[jax.ShapeDtypeStruct((batch, t, SSM_INNER), BF16),
                   jax.ShapeDtypeStruct((batch, t, SSM_INNER), BF16),
                   jax.ShapeDtypeStruct((batch, t, SSM_BC_WIDTH), BF16),
                   jax.ShapeDtypeStruct((batch, t, SSM_BC_WIDTH), BF16),
                   jax.ShapeDtypeStruct((batch, t, LANES), F32)],
        compiler_params=_cparams(2),
        name="ssm_in",
    )(h, h, h, mods, g1, w_in_all, wdt, cw, cb, dtb)


def _ssd_chunk(xs_ref, bm_ref, cm_ref, dt_ref, a_ref, ex_ref, y_ref, st_ref, *, reverse, lane_base):
    L = SSM_CHUNK

    @pl.when(pl.program_id(1) == 0)
    def _():
        st_ref[...] = jnp.zeros(st_ref.shape, F32)

    dt = dt_ref[...]
    da = dt * a_ref[...]
    r = lax.broadcasted_iota(jnp.int32, (L, L), 0)
    c = lax.broadcasted_iota(jnp.int32, (L, L), 1)
    before = (c >= r) if reverse else (c <= r)
    tri = jnp.where(before, 1.0, 0.0).astype(BF16)
    hi = da.astype(BF16)
    r1 = da - hi.astype(F32)
    mid = r1.astype(BF16)
    lo = (r1 - mid.astype(F32)).astype(BF16)
    p3 = jnp.dot(tri, jnp.concatenate([hi, mid, lo], axis=1), preferred_element_type=F32)
    P = p3[:, 0:LANES] + p3[:, LANES:2 * LANES] + p3[:, 2 * LANES:3 * LANES]
    tot = jnp.sum(da, axis=0, keepdims=True)
    QT = (P - jnp.log(dt)).T

    def expand(v):
        return jnp.dot(jnp.concatenate(_split2(v), axis=1), ex_ref[...], preferred_element_type=F32)

    w_state_all = expand(dt * jnp.exp(tot - P))
    e_p_all = expand(jnp.exp(P))
    end = 0 if reverse else L - 1
    lane = lax.broadcasted_iota(jnp.int32, (1, LANES), 1)
    first_head = jnp.where(lane < SSM_HEAD_DIM, 1.0, 0.0).astype(BF16)
    second_head = jnp.where(lane >= SSM_HEAD_DIM, 1.0, 0.0).astype(BF16)
    heads_per_group = SSM_HEADS // SSM_GROUPS
    gw = heads_per_group * SSM_HEAD_DIM
    for g in range(SSM_GROUPS):
        e_p_x = e_p_all[:, g * gw:(g + 1) * gw]
        bg = bm_ref[:, g * SSM_STATE:(g + 1) * SSM_STATE]
        cg = cm_ref[:, g * SSM_STATE:(g + 1) * SSM_STATE]
        cb = lax.dot_general(cg, bg, (((1,), (1,)), ((), ())), preferred_element_type=F32)
        st = st_ref[g]
        y_off = jnp.dot(cg, st.astype(BF16), preferred_element_type=F32) * e_p_x
        ys = []
        for pair in range(heads_per_group // 2):
            ms = []
            for k in range(2):
                hl = lane_base + g * heads_per_group + 2 * pair + k
                seg = P[:, hl:hl + 1] - QT[hl:hl + 1, :]
                ms.append((cb * jnp.exp(jnp.where(before, seg, -jnp.inf))).astype(BF16))
            xp = xs_ref[:, g * gw + pair * LANES:g * gw + (pair + 1) * LANES]
            rhs = jnp.concatenate([xp * first_head, xp * second_head], axis=0)
            ys.append(jnp.dot(jnp.concatenate(ms, axis=1), rhs, preferred_element_type=F32))
        y_ref[:, g * gw:(g + 1) * gw] = (jnp.concatenate(ys, axis=1) + y_off).astype(BF16)
        xdd = xs_ref[:, g * gw:(g + 1) * gw] * w_state_all[:, g * gw:(g + 1) * gw].astype(BF16)
        bgT = bg.astype(F32).T.astype(BF16)
        st_ref[g] = st * e_p_x[end:end + 1, :] + jnp.dot(bgT, xdd, preferred_element_type=F32)


def _ssd_pair_kernel(xf_ref, bf_ref, cf_ref, dtf_ref, xb_ref, bb_ref, cb_ref, dtb_ref, a_ref, ex_ref,
                     yf_ref, yb_ref, stf_ref, stb_ref):
    _ssd_chunk(xf_ref, bf_ref, cf_ref, dtf_ref, a_ref.at[0], ex_ref.at[0], yf_ref, stf_ref,
               reverse=False, lane_base=0)
    _ssd_chunk(xb_ref, bb_ref, cb_ref, dtb_ref, a_ref.at[1], ex_ref.at[1], yb_ref, stb_ref,
               reverse=True, lane_base=SSM_HEADS)


def _ssd(xs, bm, cm, dt, a_rows, ex, n_ctx_chunks):
    batch, t, _ = xs.shape
    L = SSM_CHUNK
    nc = t // L
    fwd = lambda i: i
    bwd = lambda i: jnp.where(i < n_ctx_chunks, n_ctx_chunks - 1 - i, nc - 1 + n_ctx_chunks - i)
    tok = lambda n, cmap: pl.BlockSpec((None, L, n), lambda b, i: (b, cmap(i), 0))
    chunk_specs = lambda cmap: [tok(SSM_INNER, cmap), tok(SSM_BC_WIDTH, cmap), tok(SSM_BC_WIDTH, cmap),
                                tok(LANES, cmap)]
    state = pltpu.VMEM((SSM_GROUPS, SSM_STATE, SSM_INNER // SSM_GROUPS), F32)
    y_shape = jax.ShapeDtypeStruct((batch, t, SSM_INNER), BF16)
    return pl.pallas_call(
        _ssd_pair_kernel,
        grid=(batch, nc),
        in_specs=(chunk_specs(fwd) + chunk_specs(bwd)
                  + [_const_spec((2, 1, LANES)), _const_spec((2, 2 * LANES, SSM_INNER))]),
        out_specs=[tok(SSM_INNER, fwd), tok(SSM_INNER, bwd)],
        out_shape=[y_shape, y_shape],
        scratch_shapes=[state, state],
        compiler_params=_cparams(2),
        name="ssd",
    )(xs, bm, cm, dt, xs, bm, cm, dt, a_rows, ex)


def _rope_tables(ctx_len, seq):
    tpos = jnp.arange(seq)
    row = (tpos // GRID_W).astype(F32)
    col = (tpos % GRID_W).astype(F32)
    n_freq = HEAD_DIM // 4
    inv = ROPE_THETA ** (-jnp.arange(n_freq, dtype=F32) / n_freq)
    ang = jnp.concatenate([row[:, None] * inv, col[:, None] * inv], axis=-1)
    cos = jnp.cos(ang)
    sin = jnp.sin(ang)
    cos64 = jnp.concatenate([cos, cos], axis=-1)
    sin64 = jnp.concatenate([-sin, sin], axis=-1)
    cos_t = jnp.concatenate([jnp.ones((ctx_len, HEAD_DIM), F32), cos64], axis=0)
    sin_t = jnp.concatenate([jnp.zeros((ctx_len, HEAD_DIM), F32), sin64], axis=0)
    return jnp.tile(cos_t, (1, N_KV_HEADS)), jnp.tile(sin_t, (1, N_KV_HEADS))


def _block_mean_matrix(n, group):
    idx = np.arange(n) // group
    return jnp.asarray((idx[:, None] == idx[None, :]).astype(np.float32) / group, dtype=BF16)


def kernel(x, c, ctx, c_ctx, ada_w, ada_b, norm1_g, norm2_g, ffn_w_in, ffn_w_out, mix_w_in, mix_w_out,
           q_norm_g, k_norm_g, sgu_norm_g, sgu_w, sgu_b, ssm_w_in, ssm_conv_w, ssm_conv_b, ssm_dt_bias,
           ssm_a_log, ssm_d, ssm_norm_g, ssm_w_out):
    batch, seq, _ = x.shape
    ctx_len = ctx.shape[1]
    depth = ada_w.shape[0]
    assert ctx_len % TOKEN_TILE == 0 and seq % TOKEN_TILE == 0 and batch < SUBLANES
    n_ctx_tiles = ctx_len // TOKEN_TILE
    n_ctx_chunks = ctx_len // SSM_CHUNK
    assert ssm_conv_w.shape[1] == 3

    h = jnp.concatenate([ctx, x], axis=1)
    cond = jnp.zeros((SUBLANES, D_MODEL), F32).at[:batch].set(c).at[batch].set(c_ctx)
    mods = _adaln_all(cond, ada_w, ada_b)[:, :, None, :]

    cos_t, sin_t = _rope_tables(ctx_len, seq)
    gq = _block_mean_matrix(ATTN_WIDTH, HEAD_DIM)
    gk = _block_mean_matrix(KV_WIDTH, HEAD_DIM)
    perm = np.arange(ATTN_WIDTH).reshape(N_KV_HEADS, Q_PER_KV, HEAD_DIM).transpose(1, 0, 2).reshape(-1)
    ex_np = np.zeros((2, 2 * LANES, SSM_INNER), np.float32)
    for d in range(2):
        for hd in range(SSM_HEADS):
            for half in range(2):
                ex_np[d, half * LANES + d * SSM_HEADS + hd, hd * SSM_HEAD_DIM:(hd + 1) * SSM_HEAD_DIM] = 1.0
    ex = jnp.asarray(ex_np, dtype=BF16)

    ffn_w_in_b = ffn_w_in.astype(BF16)
    ffn_w_out_b = ffn_w_out.astype(BF16)
    mix_w_in_b = mix_w_in.astype(BF16)
    ssm_w_in_b = ssm_w_in.astype(BF16)
    ssm_w_out_b = ssm_w_out.astype(BF16)
    norm2_rows = norm2_g[:, None, :]
    for i in range(depth):
        j = i // 2
        m = mods[i]
        g1 = norm1_g[i][None, :]
        last = i == depth - 1
        ffn_consts = [(norm2_rows, i), (ffn_w_in_b, i), (ffn_w_out_b, i)]
        if i % 2 == 0:
            qT, k, vT, gu, vn = _mix_in(
                h, m, g1, mix_w_in_b, j, jnp.tile(q_norm_g[j], N_Q_HEADS)[None, :],
                jnp.tile(k_norm_g[j], N_KV_HEADS)[None, :], cos_t, sin_t, gq, gk,
                sgu_norm_g[j][None, :], n_ctx_tiles)
            score_bound = (1.02 * HEAD_DIM ** 0.5 * LOG2E) * jnp.max(jnp.abs(q_norm_g[j])) * jnp.max(jnp.abs(k_norm_g[j]))
            safe = (score_bound <= SAFE_LOG2_BOUND).astype(jnp.int32).reshape(1)
            att = _attention(safe, qT, k, vT, n_ctx_tiles, ctx_len)
            w_out = mix_w_out[j]
            w_out = jnp.concatenate([w_out[:ATTN_WIDTH][perm], w_out[ATTN_WIDTH:]], axis=0).astype(BF16)
            sb = jnp.repeat(sgu_b[j].T, GMLP_WIDTH // GMLP_GROUPS, axis=1)
            h = _layer_tail(_mix_project, "mix_tail", [att, gu, vn], h, m,
                            [sgu_w[j].astype(BF16), sb, w_out], ffn_consts, n_ctx_tiles, last)
        else:
            n_dt = 2 * SSM_HEADS
            wdt = jnp.zeros((D_MODEL, LANES), BF16).at[:, :n_dt].set(ssm_w_in_b[j, :, SSM_INNER + SSM_CONV_DIM:])
            dtb = jnp.zeros((1, LANES), F32).at[0, :n_dt].set(ssm_dt_bias[j].reshape(-1))
            cw = jnp.zeros((SUBLANES, SSM_CONV_DIM), F32).at[:ssm_conv_w.shape[1]].set(ssm_conv_w[j])
            z, xs, bm, cm, dt = _ssm_in(h, m, g1, ssm_w_in_b, j, wdt, cw, ssm_conv_b[j][None, :], dtb, n_ctx_tiles)
            a = -jnp.exp(ssm_a_log[j].astype(F32))
            a_rows = jnp.zeros((2, 1, LANES), F32)
            for d in range(2):
                a_rows = a_rows.at[d, 0, d * SSM_HEADS:(d + 1) * SSM_HEADS].set(a[d])
            ys = _ssd(xs, bm, cm, dt, a_rows, ex, n_ctx_chunks)
            dsk = jnp.repeat(ssm_d[j], SSM_HEAD_DIM)[None, :]
            h = _layer_tail(_ssm_project, "ssm_tail", [ys[0], ys[1], xs, z], h, m,
                            [dsk, ssm_norm_g[j][None, :], (ssm_w_out_b, j)], ffn_consts,
                            n_ctx_tiles, last)
    return h
```

```python
import functools
import math

import jax
import jax.numpy as jnp
import numpy as np
from jax import lax
from jax.experimental import pallas as pl
from jax.experimental.pallas import tpu as pltpu

F32 = jnp.float32
BF16 = jnp.bfloat16

D_MODEL = 1024
GRID_W = 64
HEAD_DIM = 64
ATTN_WIDTH = D_MODEL // 2
N_Q_HEADS = ATTN_WIDTH // HEAD_DIM
N_KV_HEADS = 2
Q_PER_KV = N_Q_HEADS // N_KV_HEADS
KV_WIDTH = N_KV_HEADS * HEAD_DIM
ROPE_THETA = 10000.0
GMLP_WIDTH = D_MODEL // 2
GMLP_CHUNK = 128
GMLP_GROUPS = 4
MIX_IN = ATTN_WIDTH + 2 * KV_WIDTH + 2 * GMLP_WIDTH
SSM_INNER = 2 * D_MODEL
SSM_HEAD_DIM = 64
SSM_HEADS = SSM_INNER // SSM_HEAD_DIM
SSM_GROUPS = 4
SSM_STATE = 128
SSM_CHUNK = 128
SSM_BC_WIDTH = SSM_GROUPS * SSM_STATE
SSM_CONV_DIM = SSM_INNER + 2 * SSM_BC_WIDTH
FFN_HIDDEN = -(-8 * D_MODEL // (3 * 256)) * 256
NORM_EPS = 1e-6

LANES = 128
SUBLANES = 8
TOKEN_TILE = 256
VMEM_LIMIT = 56 * 1024 * 1024
LOG2E = 1.4426950408889634
BF16_ROWS = 16
V_ROWS = HEAD_DIM + BF16_ROWS
SAFE_LOG2_BOUND = 60.0


def _cparams(n_axes):
    return pltpu.CompilerParams(dimension_semantics=("arbitrary",) * n_axes,
                                vmem_limit_bytes=VMEM_LIMIT)


def _silu(x):
    return x / (1.0 + jnp.exp2(x * (-LOG2E)))


def _gelu_tanh(x):
    c = math.sqrt(2.0 / math.pi)
    return x * (0.5 * (1.0 + jnp.tanh(c * (x + 0.044715 * (x * x * x)))))


def _rms_mod(x, g, shift, scale):
    ms = jnp.mean(x * x, axis=-1, keepdims=True)
    return (x * lax.rsqrt(ms + NORM_EPS)) * (g * (1.0 + scale)) + shift


def _split2(x):
    hi = x.astype(BF16)
    lo = (x - hi.astype(F32)).astype(BF16)
    return hi, lo


def _group_mean(x2, gmat):
    hi, lo = _split2(x2)
    return (jnp.dot(hi, gmat, preferred_element_type=F32)
            + jnp.dot(lo, gmat, preferred_element_type=F32))


def _rope(x, cos, sin_signed):
    n = x.shape[-1]
    half = HEAD_DIM // 2
    fwd = pltpu.roll(x, n - half, axis=1)
    bwd = pltpu.roll(x, half, axis=1)
    lane = lax.broadcasted_iota(jnp.int32, x.shape, 1)
    partner = jnp.where((lane % HEAD_DIM) < half, fwd, bwd)
    return x * cos + partner * sin_signed


def _adaln_kernel(c_ref, w_ref, b_ref, o_ref):
    s = _silu(c_ref[...]).astype(BF16)
    o_ref[...] = jnp.dot(s, w_ref[...].astype(BF16), preferred_element_type=F32) + b_ref[...]


def _adaln_all(cond, ada_w, ada_b):
    depth = ada_w.shape[0]
    tn = 1536
    return pl.pallas_call(
        _adaln_kernel,
        grid=(depth, 6 * D_MODEL // tn),
        in_specs=[pl.BlockSpec((SUBLANES, D_MODEL), lambda l, j: (0, 0)),
                  pl.BlockSpec((None, D_MODEL, tn), lambda l, j: (l, 0, j)),
                  pl.BlockSpec((None, 1, tn), lambda l, j: (l, 0, j))],
        out_specs=pl.BlockSpec((None, SUBLANES, tn), lambda l, j: (l, 0, j)),
        out_shape=jax.ShapeDtypeStruct((depth, SUBLANES, 6 * D_MODEL), F32),
        compiler_params=_cparams(2),
        name="adaln",
    )(cond, ada_w, ada_b.reshape(depth, 1, 6 * D_MODEL))


def _mod_spec(n_ctx_tiles, batch):
    return pl.BlockSpec((None, 1, 6 * D_MODEL),
                        lambda b, i: (jnp.where(i < n_ctx_tiles, batch, b), 0, 0))


def _const_spec(shape):
    nd = len(shape)
    return pl.BlockSpec(shape, lambda b, i: (0,) * nd)


def _layer_spec(stacked, layer, **kwargs):
    idx = (layer,) + (0,) * (stacked.ndim - 1)
    return pl.BlockSpec((None,) + stacked.shape[1:], lambda b, i: idx, **kwargs)


def _mix_in_kernel(x_ref, mod_ref, g1_ref, w_ref, qg_ref, kg_ref, cos_ref, sin_ref, gq_ref, gk_ref,
                   sg_ref, qT_ref, k_ref, vT_ref, gu_ref, vn_ref):
    mod = mod_ref[...]
    h = _rms_mod(x_ref[...], g1_ref[...], mod[:, 0:D_MODEL], mod[:, D_MODEL:2 * D_MODEL]).astype(BF16)
    p = jnp.dot(h, w_ref[...], preferred_element_type=F32)
    q = p[:, 0:ATTN_WIDTH]
    k = p[:, ATTN_WIDTH:ATTN_WIDTH + KV_WIDTH]
    v = p[:, ATTN_WIDTH + KV_WIDTH:ATTN_WIDTH + 2 * KV_WIDTH]
    gm = p[:, ATTN_WIDTH + 2 * KV_WIDTH:]
    cos = cos_ref[...]
    sin = sin_ref[...]
    q = q * lax.rsqrt(_group_mean(q * q, gq_ref[...]) + NORM_EPS) * qg_ref[...]
    k = k * lax.rsqrt(_group_mean(k * k, gk_ref[...]) + NORM_EPS) * kg_ref[...]
    reps = ATTN_WIDTH // KV_WIDTH
    q = _rope(q, jnp.concatenate([cos] * reps, axis=1), jnp.concatenate([sin] * reps, axis=1))
    k = _rope(k, cos, sin)
    q = q * (HEAD_DIM ** -0.5 * LOG2E)
    qT_ref[...] = q.T.astype(BF16)
    k_ref[...] = k.astype(BF16)
    vT = v.T
    ones = jnp.ones((BF16_ROWS, vT.shape[1]), F32)
    vT_ref[...] = jnp.concatenate(
        [blk for kvh in range(N_KV_HEADS) for blk in (vT[kvh * HEAD_DIM:(kvh + 1) * HEAD_DIM, :], ones)],
        axis=0).astype(BF16)
    ge = _gelu_tanh(gm)
    gu_ref[...] = ge[:, 0:GMLP_WIDTH].astype(BF16)
    gv = ge[:, GMLP_WIDTH:]
    gd = GMLP_WIDTH // GMLP_GROUPS
    parts = []
    for g in range(GMLP_GROUPS):
        blk = gv[:, g * gd:(g + 1) * gd]
        ms = jnp.mean(blk * blk, axis=-1, keepdims=True)
        parts.append(blk * lax.rsqrt(ms + NORM_EPS))
    vn_ref[...] = (jnp.concatenate(parts, axis=1) * sg_ref[...]).astype(BF16)


def _mix_in(h, mods, g1, w_in_all, layer, qg, kg, cos, sin, gq, gk, sgg, n_ctx_tiles):
    batch, t, _ = h.shape
    tm = TOKEN_TILE
    tok = lambda n: pl.BlockSpec((None, tm, n), lambda b, i: (b, i, 0))
    tr = lambda n: pl.BlockSpec((None, n, tm), lambda b, i: (b, 0, i))
    return pl.pallas_call(
        _mix_in_kernel,
        grid=(batch, t // tm),
        in_specs=[tok(D_MODEL), _mod_spec(n_ctx_tiles, batch), _const_spec((1, D_MODEL)),
                  _layer_spec(w_in_all, layer), _const_spec((1, ATTN_WIDTH)), _const_spec((1, KV_WIDTH)),
                  pl.BlockSpec((tm, KV_WIDTH), lambda b, i: (i, 0)),
                  pl.BlockSpec((tm, KV_WIDTH), lambda b, i: (i, 0)),
                  _const_spec((ATTN_WIDTH, ATTN_WIDTH)), _const_spec((KV_WIDTH, KV_WIDTH)),
                  _const_spec((1, GMLP_WIDTH))],
        out_specs=[tr(ATTN_WIDTH), tok(KV_WIDTH), tr(N_KV_HEADS * V_ROWS), tok(GMLP_WIDTH), tok(GMLP_WIDTH)],
        out_shape=[jax.ShapeDtypeStruct((batch, ATTN_WIDTH, t), BF16),
                   jax.ShapeDtypeStruct((batch, t, KV_WIDTH), BF16),
                   jax.ShapeDtypeStruct((batch, N_KV_HEADS * V_ROWS, t), BF16),
                   jax.ShapeDtypeStruct((batch, t, GMLP_WIDTH), BF16),
                   jax.ShapeDtypeStruct((batch, t, GMLP_WIDTH), BF16)],
        compiler_params=_cparams(2),
        name="mix_in",
    )(h, mods, g1, w_in_all, qg, kg, cos, sin, gq, gk, sgg)


def _attn_kernel(safe_ref, qT_ref, k_ref, vT_ref, o_ref, qpad_ref, acc_ref, m_ref, *, tq, tk, tk_main,
                 n_ctx_tiles, ctx_len, t):
    i = pl.program_id(1)
    for kvh in range(N_KV_HEADS):
        qcat = jnp.concatenate(
            [qT_ref[(kvh * Q_PER_KV + g) * HEAD_DIM:(kvh * Q_PER_KV + g + 1) * HEAD_DIM, :]
             for g in range(Q_PER_KV)], axis=1)
        z = jnp.zeros_like(qcat)
        qpad_ref[:, kvh * Q_PER_KV * tq:(kvh + 1) * Q_PER_KV * tq] = jnp.concatenate(
            [qcat, z] if kvh == 0 else [z, qcat], axis=0)
    acc_ref[...] = jnp.zeros(acc_ref.shape, F32)
    is_latent = i >= n_ctx_tiles
    safe = safe_ref[0] != 0

    def plain_step(off, width):
        kblk = k_ref[pl.ds(off, width), :]
        p = jnp.exp2(jnp.dot(kblk, qpad_ref[...], preferred_element_type=F32)).astype(BF16)
        for kvh in range(N_KV_HEADS):
            vblk = vT_ref[kvh * V_ROWS:(kvh + 1) * V_ROWS, pl.ds(off, width)]
            acc_ref[kvh] += jnp.dot(vblk, p[:, kvh * Q_PER_KV * tq:(kvh + 1) * Q_PER_KV * tq],
                                    preferred_element_type=F32)

    @pl.when(safe)
    def _():
        plain_step(0, ctx_len)

        @pl.when(is_latent)
        def _():
            def body(j, carry):
                plain_step(pl.multiple_of(ctx_len + j * tk_main, LANES), tk_main)
                return carry
            lax.fori_loop(0, (t - ctx_len) // tk_main, body, 0, unroll=4)

    @pl.when(jnp.logical_not(safe))
    def _():
        m_ref[...] = jnp.full(m_ref.shape, -jnp.inf, F32)

        def body(j, carry):
            off = pl.multiple_of(j * tk, tk)
            kblk = k_ref[pl.ds(off, tk), :]
            for kvh in range(N_KV_HEADS):
                s = jnp.dot(kblk, qpad_ref[:, kvh * Q_PER_KV * tq:(kvh + 1) * Q_PER_KV * tq],
                            preferred_element_type=F32)
                m_old = m_ref[kvh]
                m_new = jnp.maximum(m_old, jnp.max(s, axis=0, keepdims=True))
                alpha = jnp.exp2(m_old - m_new)
                p = jnp.exp2(s - m_new).astype(BF16)
                vblk = vT_ref[kvh * V_ROWS:(kvh + 1) * V_ROWS, pl.ds(off, tk)]
                acc_ref[kvh] = acc_ref[kvh] * alpha + jnp.dot(vblk, p, preferred_element_type=F32)
                m_ref[kvh] = m_new
            return carry

        lax.fori_loop(0, jnp.where(is_latent, t // tk, ctx_len // tk), body, 0)

    for g in range(Q_PER_KV):
        og = jnp.concatenate(
            [acc_ref[kvh, 0:HEAD_DIM, g * tq:(g + 1) * tq]
             / acc_ref[kvh, HEAD_DIM:HEAD_DIM + 1, g * tq:(g + 1) * tq]
             for kvh in range(N_KV_HEADS)], axis=0)
        o_ref[:, g * KV_WIDTH:(g + 1) * KV_WIDTH] = og.T.astype(BF16)


def _attention(safe, qT, k, vT, n_ctx_tiles, ctx_len):
    batch, _, t = qT.shape
    tq = TOKEN_TILE
    tk = 256
    tk_main = 2048
    assert ctx_len % tk == 0 and t % tk == 0 and (t - ctx_len) % tk_main == 0
    kern = functools.partial(_attn_kernel, tq=tq, tk=tk, tk_main=tk_main, n_ctx_tiles=n_ctx_tiles,
                             ctx_len=ctx_len, t=t)
    return pl.pallas_call(
        kern,
        grid=(batch, t // tq),
        in_specs=[pl.BlockSpec(memory_space=pltpu.SMEM),
                  pl.BlockSpec((None, ATTN_WIDTH, tq), lambda b, i: (b, 0, i)),
                  pl.BlockSpec((None, t, KV_WIDTH), lambda b, i: (b, 0, 0)),
                  pl.BlockSpec((None, N_KV_HEADS * V_ROWS, t), lambda b, i: (b, 0, 0))],
        out_specs=pl.BlockSpec((None, tq, ATTN_WIDTH), lambda b, i: (b, i, 0)),
        out_shape=jax.ShapeDtypeStruct((batch, t, ATTN_WIDTH), BF16),
        scratch_shapes=[pltpu.VMEM((KV_WIDTH, N_KV_HEADS * Q_PER_KV * tq), BF16),
                        pltpu.VMEM((N_KV_HEADS, V_ROWS, Q_PER_KV * tq), F32),
                        pltpu.VMEM((N_KV_HEADS, 1, Q_PER_KV * tq), F32)],
        compiler_params=_cparams(2),
        name="attention",
    )(safe, qT, k, vT)


FFN_SPLIT = 1


def _mix_project(att_ref, gu_ref, vn_ref, sw_ref, sb_ref, wo_ref):
    tm = att_ref.shape[0]
    gd = GMLP_WIDTH // GMLP_GROUPS
    chunks = []
    for c in range(tm // GMLP_CHUNK):
        rows = slice(c * GMLP_CHUNK, (c + 1) * GMLP_CHUNK)
        s = jnp.concatenate(
            [jnp.dot(sw_ref[g], vn_ref[rows, g * gd:(g + 1) * gd], preferred_element_type=F32)
             for g in range(GMLP_GROUPS)], axis=1) + sb_ref[...]
        chunks.append(gu_ref[rows, :].astype(F32) * s)
    sg = jnp.concatenate(chunks, axis=0).astype(BF16)
    return (jnp.dot(att_ref[...], wo_ref[0:ATTN_WIDTH, :], preferred_element_type=F32)
            + jnp.dot(sg, wo_ref[ATTN_WIDTH:, :], preferred_element_type=F32))


def _ssm_project(yf_ref, yb_ref, xs_ref, z_ref, dsk_ref, ng_ref, wo_ref):
    gw = SSM_INNER // SSM_GROUPS
    out = None
    for g in range(SSM_GROUPS):
        cols = slice(g * gw, (g + 1) * gw)
        y = (yf_ref[:, cols].astype(F32) + yb_ref[:, cols].astype(F32)
             + dsk_ref[:, cols] * xs_ref[:, cols].astype(F32))
        y = y * _silu(z_ref[:, cols].astype(F32))
        ms = jnp.mean(y * y, axis=-1, keepdims=True)
        yn = (y * lax.rsqrt(ms + NORM_EPS) * ng_ref[:, cols]).astype(BF16)
        part = jnp.dot(yn, wo_ref[cols, :], preferred_element_type=F32)
        out = part if out is None else out + part
    return out


def _tail_kernel(*refs, project, n_tok, n_proj):
    tok_refs = refs[:n_tok]
    x_ref, mod_ref = refs[n_tok:n_tok + 2]
    proj_refs = refs[n_tok + 2:n_tok + 2 + n_proj]
    g2_ref, wi_ref, wf_ref, o_ref = refs[n_tok + 2 + n_proj:]
    mod = mod_ref[...]
    x = x_ref[...] + mod[:, 2 * D_MODEL:3 * D_MODEL] * project(*tok_refs, *proj_refs)
    h = _rms_mod(x, g2_ref[...], mod[:, 3 * D_MODEL:4 * D_MODEL], mod[:, 4 * D_MODEL:5 * D_MODEL]).astype(BF16)
    hc = FFN_HIDDEN // FFN_SPLIT
    acc = jnp.zeros(x.shape, F32)
    for c in range(FFN_SPLIT):
        g = jnp.dot(h, wi_ref[:, c * hc:(c + 1) * hc], preferred_element_type=F32)
        u = jnp.dot(h, wi_ref[:, FFN_HIDDEN + c * hc:FFN_HIDDEN + (c + 1) * hc], preferred_element_type=F32)
        a = (_silu(g) * u).astype(BF16)
        acc = acc + jnp.dot(a, wf_ref[c * hc:(c + 1) * hc, :], preferred_element_type=F32)
    o_ref[...] = x + mod[:, 5 * D_MODEL:6 * D_MODEL] * acc


def _layer_tail(project, name, tok_inputs, h, mods, proj_consts, ffn_consts, n_ctx_tiles, latent_only):
    batch, t, _ = h.shape
    tm = TOKEN_TILE
    skip = n_ctx_tiles if latent_only else 0
    tok = lambda n: pl.BlockSpec((None, tm, n), lambda b, i: (b, i + skip, 0))
    mod_spec = pl.BlockSpec((None, 1, 6 * D_MODEL),
                            lambda b, i: (jnp.where(i + skip < n_ctx_tiles, batch, b), 0, 0))

    def resident(c):
        if isinstance(c, tuple):
            return _layer_spec(*c, pipeline_mode=pl.Buffered(1))
        return pl.BlockSpec(c.shape, lambda b, i, nd=c.ndim: (0,) * nd, pipeline_mode=pl.Buffered(1))

    consts = list(proj_consts) + list(ffn_consts)
    arrays = [c[0] if isinstance(c, tuple) else c for c in consts]
    kern = functools.partial(_tail_kernel, project=project, n_tok=len(tok_inputs), n_proj=len(proj_consts))
    return pl.pallas_call(
        kern,
        grid=(batch, t // tm - skip),
        in_specs=([tok(a.shape[-1]) for a in tok_inputs] + [tok(D_MODEL), mod_spec]
                  + [resident(a) for a in consts]),
        out_specs=pl.BlockSpec((None, tm, D_MODEL), lambda b, i: (b, i, 0)),
        out_shape=jax.ShapeDtypeStruct((batch, t - skip * tm, D_MODEL), F32),
        compiler_params=_cparams(2),
        name=name,
    )(*tok_inputs, h, mods, *arrays)


def _ssm_in_kernel(xp_ref, x_ref, xn_ref, mod_ref, g1_ref, w_ref, wdt_ref, cw_ref, cb_ref, dtb_ref,
                   z_ref, xs_ref, bm_ref, cm_ref, dt_ref, *, n_tiles, n_ctx_tiles, col_block):
    i = pl.program_id(1)
    tm = x_ref.shape[0]
    halo = SUBLANES
    prev_ok = jnp.logical_and(i != 0, i != n_ctx_tiles)
    next_ok = jnp.logical_and(i != n_ctx_tiles - 1, i != n_tiles - 1)
    xcat = jnp.concatenate([x_ref[...], xn_ref[...], xp_ref[...]], axis=0)
    mod = mod_ref[...]
    h = _rms_mod(xcat, g1_ref[...], mod[:, 0:D_MODEL], mod[:, D_MODEL:2 * D_MODEL]).astype(BF16)
    hm = h[0:tm, :]
    z_ref[...] = jnp.dot(hm, w_ref[:, 0:SSM_INNER], preferred_element_type=F32).astype(BF16)
    dtr = jnp.dot(hm, wdt_ref[...], preferred_element_type=F32) + dtb_ref[...]
    dt_ref[...] = jnp.maximum(dtr, 0.0) + jnp.log1p(jnp.exp(-jnp.abs(dtr)))
    prev_f = prev_ok.astype(F32)
    next_f = next_ok.astype(F32)
    for blk in range(SSM_CONV_DIM // col_block):
        cols = slice(blk * col_block, (blk + 1) * col_block)
        p = jnp.dot(h, w_ref[:, SSM_INNER + blk * col_block:SSM_INNER + (blk + 1) * col_block],
                    preferred_element_type=F32)
        p = jnp.concatenate([p[0:tm, :], p[tm:tm + halo, :] * next_f, p[tm + halo:, :] * prev_f], axis=0)
        pm1 = pltpu.roll(p, 1, axis=0)[0:tm, :]
        pp1 = pltpu.roll(p, tm + 2 * halo - 1, axis=0)[0:tm, :]
        y = (cw_ref[0:1, cols] * pm1 + cw_ref[1:2, cols] * p[0:tm, :]
             + cw_ref[2:3, cols] * pp1 + cb_ref[:, cols])
        y = _silu(y).astype(BF16)
        c0 = blk * col_block
        if c0 < SSM_INNER:
            xs_ref[:, c0:c0 + col_block] = y
        elif c0 < SSM_INNER + SSM_BC_WIDTH:
            bm_ref[:, c0 - SSM_INNER:c0 - SSM_INNER + col_block] = y
        else:
            o0 = c0 - SSM_INNER - SSM_BC_WIDTH
            cm_ref[:, o0:o0 + col_block] = y


def _ssm_in(h, mods, g1, w_in_all, layer, wdt, cw, cb, dtb, n_ctx_tiles):
    batch, t, _ = h.shape
    tm = TOKEN_TILE
    n_tiles = t // tm
    per = tm // SUBLANES
    tok = lambda n: pl.BlockSpec((None, tm, n), lambda b, i: (b, i, 0))
    col_block = 512
    kern = functools.partial(_ssm_in_kernel, n_tiles=n_tiles, n_ctx_tiles=n_ctx_tiles, col_block=col_block)
    return pl.pallas_call(
        kern,
        grid=(batch, n_tiles),
        in_specs=[pl.BlockSpec((None, SUBLANES, D_MODEL), lambda b, i: (b, jnp.maximum(i * per - 1, 0), 0)),
                  tok(D_MODEL),
                  pl.BlockSpec((None, SUBLANES, D_MODEL),
                               lambda b, i: (b, jnp.minimum((i + 1) * per, t // SUBLANES - 1), 0)),
                  _mod_spec(n_ctx_tiles, batch), _const_spec((1, D_MODEL)),
                  _layer_spec(w_in_all, layer),
                  _const_spec((D_MODEL, LANES)), _const_spec((SUBLANES, SSM_CONV_DIM)),
                  _const_spec((1, SSM_CONV_DIM)), _const_spec((1, LANES))],
        out_specs=[tok(SSM_INNER), tok(SSM_INNER), tok(SSM_BC_WIDTH), tok(SSM_BC_WIDTH), tok(LANES)],
        out_shape=[jax.ShapeDtypeStruct((batch, t, SSM_INNER), BF16),
                   jax.ShapeDtypeStruct((batch, t, SSM_INNER), BF16),
                   jax.ShapeDtypeStruct((batch, t, SSM_BC_WIDTH), BF16),
                   jax.ShapeDtypeStruct((batch, t, SSM_BC_WIDTH), BF16),
                   jax.ShapeDtypeStruct((batch, t, LANES), F32)],
        compiler_params=_cparams(2),
        name="ssm_in",
    )(h, h, h, mods, g1, w_in_all, wdt, cw, cb, dtb)


def _ssd_chunk(xs_ref, bm_ref, cm_ref, dt_ref, a_ref, ex_ref, y_ref, st_ref, *, reverse, lane_base):
    L = SSM_CHUNK

    @pl.when(pl.program_id(1) == 0)
    def _():
        st_ref[...] = jnp.zeros(st_ref.shape, F32)

    dt = dt_ref[...]
    da = dt * a_ref[...]
    r = lax.broadcasted_iota(jnp.int32, (L, L), 0)
    c = lax.broadcasted_iota(jnp.int32, (L, L), 1)
    before = (c >= r) if reverse else (c <= r)
    tri = jnp.where(before, 1.0, 0.0).astype(BF16)
    hi = da.astype(BF16)
    r1 = da - hi.astype(F32)
    mid = r1.astype(BF16)
    lo = (r1 - mid.astype(F32)).astype(BF16)
    p3 = jnp.dot(tri, jnp.concatenate([hi, mid, lo], axis=1), preferred_element_type=F32)
    P = p3[:, 0:LANES] + p3[:, LANES:2 * LANES] + p3[:, 2 * LANES:3 * LANES]
    tot = jnp.sum(da, axis=0, keepdims=True)
    QT = (P - jnp.log(dt)).T

    def expand(v):
        return jnp.dot(jnp.concatenate(_split2(v), axis=1), ex_ref[...], preferred_element_type=F32)

    w_state_all = expand(dt * jnp.exp(tot - P))
    e_p_all = expand(jnp.exp(P))
    end = 0 if reverse else L - 1
    lane = lax.broadcasted_iota(jnp.int32, (1, LANES), 1)
    first_head = jnp.where(lane < SSM_HEAD_DIM, 1.0, 0.0).astype(BF16)
    second_head = jnp.where(lane >= SSM_HEAD_DIM, 1.0, 0.0).astype(BF16)
    heads_per_group = SSM_HEADS // SSM_GROUPS
    gw = heads_per_group * SSM_HEAD_DIM
    for g in range(SSM_GROUPS):
        e_p_x = e_p_all[:, g * gw:(g + 1) * gw]
        bg = bm_ref[:, g * SSM_STATE:(g + 1) * SSM_STATE]
        cg = cm_ref[:, g * SSM_STATE:(g + 1) * SSM_STATE]
        cb = lax.dot_general(cg, bg, (((1,), (1,)), ((), ())), preferred_element_type=F32)
        st = st_ref[g]
        y_off = jnp.dot(cg, st.astype(BF16), preferred_element_type=F32) * e_p_x
        ys = []
        for pair in range(heads_per_group // 2):
            ms = []
            for k in range(2):
                hl = lane_base + g * heads_per_group + 2 * pair + k
                seg = P[:, hl:hl + 1] - QT[hl:hl + 1, :]
                ms.append((cb * jnp.exp(jnp.where(before, seg, -jnp.inf))).astype(BF16))
            xp = xs_ref[:, g * gw + pair * LANES:g * gw + (pair + 1) * LANES]
            rhs = jnp.concatenate([xp * first_head, xp * second_head], axis=0)
            ys.append(jnp.dot(jnp.concatenate(ms, axis=1), rhs, preferred_element_type=F32))
        y_ref[:, g * gw:(g + 1) * gw] = (jnp.concatenate(ys, axis=1) + y_off).astype(BF16)
        xdd = xs_ref[:, g * gw:(g + 1) * gw] * w_state_all[:, g * gw:(g + 1) * gw].astype(BF16)
        bgT = bg.astype(F32).T.astype(BF16)
        st_ref[g] = st * e_p_x[end:end + 1, :] + jnp.dot(bgT, xdd, preferred_element_type=F32)


def _ssd_pair_kernel(xf_ref, bf_ref, cf_ref, dtf_ref, xb_ref, bb_ref, cb_ref, dtb_ref, a_ref, ex_ref,
                     yf_ref, yb_ref, stf_ref, stb_ref):
    _ssd_chunk(xf_ref, bf_ref, cf_ref, dtf_ref, a_ref.at[0], ex_ref.at[0], yf_ref, stf_ref,
               reverse=False, lane_base=0)
    _ssd_chunk(xb_ref, bb_ref, cb_ref, dtb_ref, a_ref.at[1], ex_ref.at[1], yb_ref, stb_ref,
               reverse=True, lane_base=SSM_HEADS)


def _ssd(xs, bm, cm, dt, a_rows, ex, n_ctx_chunks):
    batch, t, _ = xs.shape
    L = SSM_CHUNK
    nc = t // L
    fwd = lambda i: i
    bwd = lambda i: jnp.where(i < n_ctx_chunks, n_ctx_chunks - 1 - i, nc - 1 + n_ctx_chunks - i)
    tok = lambda n, cmap: pl.BlockSpec((None, L, n), lambda b, i: (b, cmap(i), 0))
    chunk_specs = lambda cmap: [tok(SSM_INNER, cmap), tok(SSM_BC_WIDTH, cmap), tok(SSM_BC_WIDTH, cmap),
                                tok(LANES, cmap)]
    state = pltpu.VMEM((SSM_GROUPS, SSM_STATE, SSM_INNER // SSM_GROUPS), F32)
    y_shape = jax.ShapeDtypeStruct((batch, t, SSM_INNER), BF16)
    return pl.pallas_call(
        _ssd_pair_kernel,
        grid=(batch, nc),
        in_specs=(chunk_specs(fwd) + chunk_specs(bwd)
                  + [_const_spec((2, 1, LANES)), _const_spec((2, 2 * LANES, SSM_INNER))]),
        out_specs=[tok(SSM_INNER, fwd), tok(SSM_INNER, bwd)],
        out_shape=[y_shape, y_shape],
        scratch_shapes=[state, state],
        compiler_params=_cparams(2),
        name="ssd",
    )(xs, bm, cm, dt, xs, bm, cm, dt, a_rows, ex)


def _rope_tables(ctx_len, seq):
    tpos = jnp.arange(seq)
    row = (tpos // GRID_W).astype(F32)
    col = (tpos % GRID_W).astype(F32)
    n_freq = HEAD_DIM // 4
    inv = ROPE_THETA ** (-jnp.arange(n_freq, dtype=F32) / n_freq)
    ang = jnp.concatenate([row[:, None] * inv, col[:, None] * inv], axis=-1)
    cos = jnp.cos(ang)
    sin = jnp.sin(ang)
    cos64 = jnp.concatenate([cos, cos], axis=-1)
    sin64 = jnp.concatenate([-sin, sin], axis=-1)
    cos_t = jnp.concatenate([jnp.ones((ctx_len, HEAD_DIM), F32), cos64], axis=0)
    sin_t = jnp.concatenate([jnp.zeros((ctx_len, HEAD_DIM), F32), sin64], axis=0)
    return jnp.tile(cos_t, (1, N_KV_HEADS)), jnp.tile(sin_t, (1, N_KV_HEADS))


def _block_mean_matrix(n, group):
    idx = np.arange(n) // group
    return jnp.asarray((idx[:, None] == idx[None, :]).astype(np.float32) / group, dtype=BF16)


def kernel(x, c, ctx, c_ctx, ada_w, ada_b, norm1_g, norm2_g, ffn_w_in, ffn_w_out, mix_w_in, mix_w_out,
           q_norm_g, k_norm_g, sgu_norm_g, sgu_w, sgu_b, ssm_w_in, ssm_conv_w, ssm_conv_b, ssm_dt_bias,
           ssm_a_log, ssm_d, ssm_norm_g, ssm_w_out):
    batch, seq, _ = x.shape
    ctx_len = ctx.shape[1]
    depth = ada_w.shape[0]
    assert ctx_len % TOKEN_TILE == 0 and seq % TOKEN_TILE == 0 and batch < SUBLANES
    n_ctx_tiles = ctx_len // TOKEN_TILE
    n_ctx_chunks = ctx_len // SSM_CHUNK
    assert ssm_conv_w.shape[1] == 3

    h = jnp.concatenate([ctx, x], axis=1)
    cond = jnp.zeros((SUBLANES, D_MODEL), F32).at[:batch].set(c).at[batch].set(c_ctx)
    mods = _adaln_all(cond, ada_w, ada_b)[:, :, None, :]

    cos_t, sin_t = _rope_tables(ctx_len, seq)
    gq = _block_mean_matrix(ATTN_WIDTH, HEAD_DIM)
    gk = _block_mean_matrix(KV_WIDTH, HEAD_DIM)
    perm = np.arange(ATTN_WIDTH).reshape(N_KV_HEADS, Q_PER_KV, HEAD_DIM).transpose(1, 0, 2).reshape(-1)
    ex_np = np.zeros((2, 2 * LANES, SSM_INNER), np.float32)
    for d in range(2):
        for hd in range(SSM_HEADS):
            for half in range(2):
                ex_np[d, half * LANES + d * SSM_HEADS + hd, hd * SSM_HEAD_DIM:(hd + 1) * SSM_HEAD_DIM] = 1.0
    ex = jnp.asarray(ex_np, dtype=BF16)

    ffn_w_in_b = ffn_w_in.astype(BF16)
    ffn_w_out_b = ffn_w_out.astype(BF16)
    mix_w_in_b = mix_w_in.astype(BF16)
    ssm_w_in_b = ssm_w_in.astype(BF16)
    ssm_w_out_b = ssm_w_out.astype(BF16)
    norm2_rows = norm2_g[:, None, :]
    for i in range(depth):
        j = i // 2
        m = mods[i]
        g1 = norm1_g[i][None, :]
        last = i == depth - 1
        ffn_consts = [(norm2_rows, i), (ffn_w_in_b, i), (ffn_w_out_b, i)]
        if i % 2 == 0:
            qT, k, vT, gu, vn = _mix_in(
                h, m, g1, mix_w_in_b, j, jnp.tile(q_norm_g[j], N_Q_HEADS)[None, :],
                jnp.tile(k_norm_g[j], N_KV_HEADS)[None, :], cos_t, sin_t, gq, gk,
                sgu_norm_g[j][None, :], n_ctx_tiles)
            score_bound = (1.02 * HEAD_DIM ** 0.5 * LOG2E) * jnp.max(jnp.abs(q_norm_g[j])) * jnp.max(jnp.abs(k_norm_g[j]))
            safe = (score_bound <= SAFE_LOG2_BOUND).astype(jnp.int32).reshape(1)
            att = _attention(safe, qT, k, vT, n_ctx_tiles, ctx_len)
            w_out = mix_w_out[j]
            w_out = jnp.concatenate([w_out[:ATTN_WIDTH][perm], w_out[ATTN_WIDTH:]], axis=0).astype(BF16)
            sb = jnp.repeat(sgu_b[j].T, GMLP_WIDTH // GMLP_GROUPS, axis=1)
            h = _layer_tail(_mix_project, "mix_tail", [att, gu, vn], h, m,
                            [sgu_w[j].astype(BF16), sb, w_out], ffn_consts, n_ctx_tiles, last)
        else:
            n_dt = 2 * SSM_HEADS
            wdt = jnp.zeros((D_MODEL, LANES), BF16).at[:, :n_dt].set(ssm_w_in_b[j, :, SSM_INNER + SSM_CONV_DIM:])
            dtb = jnp.zeros((1, LANES), F32).at[0, :n_dt].set(ssm_dt_bias[j].reshape(-1))
            cw = jnp.zeros((SUBLANES, SSM_CONV_DIM), F32).at[:ssm_conv_w.shape[1]].set(ssm_conv_w[j])
            z, xs, bm, cm, dt = _ssm_in(h, m, g1, ssm_w_in_b, j, wdt, cw, ssm_conv_b[j][None, :], dtb, n_ctx_tiles)
            a = -jnp.exp(ssm_a_log[j].astype(F32))
            a_rows = jnp.zeros((2, 1, LANES), F32)
            for d in range(2):
                a_rows = a_rows.at[d, 0, d * SSM_HEADS:(d + 1) * SSM_HEADS].set(a[d])
            ys = _ssd(xs, bm, cm, dt, a_rows, ex, n_ctx_chunks)
            dsk = jnp.repeat(ssm_d[j], SSM_HEAD_DIM)[None, :]
            h = _layer_tail(_ssm_project, "ssm_tail", [ys[0], ys[1], xs, z], h, m,
                            [dsk, ssm_norm_g[j][None, :], (ssm_w_out_b, j)], ffn_consts,
                            n_ctx_tiles, last)
    return h
```
